```python
import jax
import jax.numpy as jnp
from jax import lax
import numpy as np

D_MODEL = 4096
BATCH = 1
SEQ = 8192
DEPTH = 4

CTX_LEN = 256
GRID_W = 64
N_MIXERS = 3
N_GLA = (DEPTH + 2) // 3
N_RET = (DEPTH + 1) // 3
N_MLA = DEPTH // 3
EPS = 1e-6
ROPE_BASE = 10000.0
ADA_RANK = 256
D_FF = 11008
CONV_W = 3

GLA_HEADS = 8
GLA_DK = D_MODEL // 2
GLA_DV = D_MODEL
GLA_DK_HEAD = GLA_DK // GLA_HEADS
GLA_DV_HEAD = GLA_DV // GLA_HEADS
GLA_GATE_RANK = 16
GLA_TAU = 16.0
GLA_CHUNK = 64
GLA_SPLITS = [GLA_DK, 2 * GLA_DK, 2 * GLA_DK + GLA_DV, 2 * GLA_DK + 2 * GLA_DV]
GLA_IN = 2 * GLA_DK + 2 * GLA_DV + 2 * GLA_GATE_RANK

RET_HEADS = 16
RET_DK = D_MODEL
RET_DV = 2 * D_MODEL
RET_DK_HEAD = RET_DK // RET_HEADS
RET_CHUNK = 128
RET_SPLITS = [RET_DK, 2 * RET_DK, 2 * RET_DK + RET_DV]
RET_IN = 2 * RET_DK + 2 * RET_DV

MLA_HEADS = 64
MLA_Q_RANK = 1024
MLA_KV_RANK = 512
MLA_NOPE = 128
MLA_ROPE = 64
MLA_V = 128
MLA_QK_DIM = MLA_NOPE + MLA_ROPE
MLA_DOWN = MLA_Q_RANK + MLA_KV_RANK + MLA_ROPE
Q_BLOCK = 128

kernel_name = "hybrid_gla_retnet_mla_dit_prefix"


def rms_norm(x, gain=None):
    xf = x.astype(jnp.float32)
    y = xf * lax.rsqrt(jnp.mean(xf * xf, axis=-1, keepdims=True) + EPS)
    if gain is not None:
        y = y * gain.astype(jnp.float32)
    return y.astype(x.dtype)


def modulate(x, shift, scale):
    return rms_norm(x) * (1 + scale) + shift


def ada_modulation(cond, down, up):
    m = (jax.nn.silu(cond) @ down) @ up
    return jnp.split(m[:, None, :], 6, axis=-1)


def split_heads(t, n_heads):
    b, t_len, w = t.shape
    return t.reshape(b, t_len, n_heads, w // n_heads).transpose(0, 2, 1, 3)


def merge_heads(t):
    b, h, t_len, d = t.shape
    return t.transpose(0, 2, 1, 3).reshape(b, t_len, h * d)


def flip_t(t):
    return jnp.flip(t, axis=-2)


def axial_rope(n_tokens, dim):
    rows = n_tokens // GRID_W
    quarter = dim // 4
    inv_freq = ROPE_BASE ** (-jnp.arange(quarter, dtype=jnp.float32) / quarter)
    row = jnp.repeat(jnp.arange(rows, dtype=jnp.float32), GRID_W)
    col = jnp.tile(jnp.arange(GRID_W, dtype=jnp.float32), rows)
    ang = jnp.concatenate([row[:, None] * inv_freq, col[:, None] * inv_freq], axis=-1)
    return jnp.cos(ang), jnp.sin(ang)


def apply_rope(x, rope):
    cos, sin = rope
    half = x.shape[-1] // 2
    xf = x.astype(jnp.float32)
    x1, x2 = xf[..., :half], xf[..., half:]
    return jnp.concatenate([x1 * cos - x2 * sin, x1 * sin + x2 * cos], axis=-1).astype(x.dtype)


def gla_chunked(q, k, v, log_a, s0, with_out):
    b, h, t_len, _ = q.shape
    n = t_len // GLA_CHUNK

    def chunks(t):
        return jnp.moveaxis(t.reshape(b, h, n, GLA_CHUNK, t.shape[-1]), 2, 0)

    cum = jnp.cumsum(chunks(log_a.astype(jnp.float32)), axis=-2)
    causal = jnp.tril(jnp.ones((GLA_CHUNK, GLA_CHUNK), dtype=bool))[:, :, None]

    def step(s, xs):
        qc, kc, vc, bc = xs
        b_end = bc[..., -1:, :]
        s_new = jnp.exp(b_end)[..., 0, :, None] * s + jnp.einsum('bhsk,bhsv->bhkv', kc * jnp.exp(b_end - bc), vc)
        if not with_out:
            return s_new, None
        decay = jnp.exp(jnp.where(causal, bc[..., :, None, :] - bc[..., None, :, :], -jnp.inf))
        scores = jnp.einsum('bhtk,bhsk,bhtsk->bhts', qc, kc, decay)
        o = jnp.einsum('bhts,bhsv->bhtv', scores, vc) + jnp.einsum('bhtk,bhkv->bhtv', qc * jnp.exp(bc), s)
        return s_new, o

    s_end, o = lax.scan(step, s0, (chunks(q), chunks(k), chunks(v), cum))
    if with_out:
        o = jnp.moveaxis(o, 0, 2).reshape(b, h, t_len, v.shape[-1])
    return o, s_end


def gla_project(h, w_in, w_a2, b_a):
    b, t_len, _ = h.shape
    q, k, v, r, a = jnp.split(h @ w_in, GLA_SPLITS, axis=-1)
    a = a.reshape(b, t_len, 2, GLA_GATE_RANK)
    z = jnp.einsum('btdr,drk->dbtk', a, w_a2) + b_a[:, None, None, :]
    log_a = jax.nn.log_sigmoid(z.astype(jnp.float32)) / GLA_TAU
    log_a = log_a.reshape(2, b, t_len, GLA_HEADS, GLA_DK_HEAD).transpose(0, 1, 3, 2, 4)
    q = split_heads(q, GLA_HEADS) * (GLA_DK_HEAD ** -0.5)
    return q, split_heads(k, GLA_HEADS), split_heads(v, GLA_HEADS), r, log_a


def gla_output(o, r, b_r, norm_g, w_out):
    o = merge_heads(rms_norm(o, norm_g)).astype(r.dtype)
    return (jax.nn.silu(r + b_r) * o) @ w_out


def gla_mixer(h_lat, h_ctx, w_in, w_a2, b_a, b_r, norm_g, w_out, need_ctx):
    ql, kl, vl, rl, al = gla_project(h_lat, w_in, w_a2, b_a)
    qc, kc, vc, rc, ac = gla_project(h_ctx, w_in, w_a2, b_a)
    s0 = jnp.zeros((h_lat.shape[0], GLA_HEADS, GLA_DK_HEAD, GLA_DV_HEAD), jnp.float32)
    oc_f, s_f = gla_chunked(qc, kc, vc, ac[0], s0, need_ctx)
    oc_b, s_b = gla_chunked(flip_t(qc), flip_t(kc), flip_t(vc), flip_t(ac[1]), s0, need_ctx)
    ol_f, _ = gla_chunked(ql, kl, vl, al[0], s_f, True)
    ol_b, _ = gla_chunked(flip_t(ql), flip_t(kl), flip_t(vl), flip_t(al[1]), s_b, True)
    y_lat = gla_output(ol_f + flip_t(ol_b), rl, b_r, norm_g, w_out)
    y_ctx = gla_output(oc_f + flip_t(oc_b), rc, b_r, norm_g, w_out) if need_ctx else None
    return y_lat, y_ctx


def retention_chunked(q, k, v, log_gamma, s0, with_out):
    b, h, t_len, _ = q.shape
    n = t_len // RET_CHUNK
    pos = jnp.arange(RET_CHUNK, dtype=jnp.float32)
    lg = log_gamma[:, None, None]
    rel = pos[:, None] - pos[None, :]
    decay = jnp.exp(jnp.where(rel >= 0, lg * rel, -jnp.inf))
    q_decay = jnp.exp(lg * (pos + 1.0)[:, None])
    k_decay = jnp.exp(lg * (RET_CHUNK - 1.0 - pos)[:, None])
    state_decay = jnp.exp(lg * RET_CHUNK)

    def chunks(t):
        return jnp.moveaxis(t.reshape(b, h, n, RET_CHUNK, t.shape[-1]), 2, 0)

    def step(s, xs):
        qc, kc, vc = xs
        s_new = state_decay * s + jnp.einsum('bhsk,bhsv->bhkv', kc * k_decay, vc)
        if not with_out:
            return s_new, None
        scores = jnp.einsum('bhtk,bhsk->bhts', qc, kc) * decay
        o = jnp.einsum('bhts,bhsv->bhtv', scores, vc) + jnp.einsum('bhtk,bhkv->bhtv', qc * q_decay, s)
        return s_new, o

    s_end, o = lax.scan(step, s0, (chunks(q), chunks(k), chunks(v)))
    if with_out:
        o = jnp.moveaxis(o, 0, 2).reshape(b, h, t_len, v.shape[-1])
    return o, s_end


def ret_project(h, w_in, rope):
    q, k, v, g = jnp.split(h @ w_in, RET_SPLITS, axis=-1)
    q = split_heads(q, RET_HEADS) * (RET_DK_HEAD ** -0.5)
    k = split_heads(k, RET_HEADS)
    if rope is not None:
        q, k = apply_rope(q, rope), apply_rope(k, rope)
    return q, k, split_heads(v, RET_HEADS), g


def ret_output(o, g, gn_w, gn_b, w_out):
    mu = jnp.mean(o, axis=-1, keepdims=True)
    var = jnp.mean(jnp.square(o - mu), axis=-1, keepdims=True)
    o = merge_heads((o - mu) * lax.rsqrt(var + EPS)) * gn_w + gn_b
    return (jax.nn.silu(g) * o.astype(g.dtype)) @ w_out


def retention_mixer(h_lat, h_ctx, w_in, decay_exp, gn_w, gn_b, w_out, rope, need_ctx):
    log_gamma = jnp.log1p(-jnp.exp2(-decay_exp.astype(jnp.float32)))
    ql, kl, vl, gl = ret_project(h_lat, w_in, rope)
    qc, kc, vc, gc = ret_project(h_ctx, w_in, None)
    s0 = jnp.zeros((h_lat.shape[0], RET_HEADS, RET_DK_HEAD, RET_DV // RET_HEADS), jnp.float32)
    oc_f, s_f = retention_chunked(qc, kc, vc, log_gamma[0], s0, need_ctx)
    oc_b, s_b = retention_chunked(flip_t(qc), flip_t(kc), flip_t(vc), log_gamma[1], s0, need_ctx)
    ol_f, _ = retention_chunked(ql, kl, vl, log_gamma[0], s_f, True)
    ol_b, _ = retention_chunked(flip_t(ql), flip_t(kl), flip_t(vl), log_gamma[1], s_b, True)
    y_lat = ret_output(ol_f + flip_t(ol_b), gl, gn_w, gn_b, w_out)
    y_ctx = ret_output(oc_f + flip_t(oc_b), gc, gn_w, gn_b, w_out) if need_ctx else None
    return y_lat, y_ctx


def mla_project(h, w_down, q_norm, kv_norm, w_uq, w_ukv, q_gain, k_gain, rope):
    b, t_len, _ = h.shape
    cq, ckv, k_rope = jnp.split(h @ w_down, [MLA_Q_RANK, MLA_Q_RANK + MLA_KV_RANK], axis=-1)
    q = split_heads(rms_norm(cq, q_norm) @ w_uq, MLA_HEADS)
    k_nope, v = jnp.split(split_heads(rms_norm(ckv, kv_norm) @ w_ukv, MLA_HEADS), [MLA_NOPE], axis=-1)
    k_rope = jnp.broadcast_to(k_rope[:, None], (b, MLA_HEADS, t_len, MLA_ROPE))
    q = rms_norm(q, q_gain)
    k = rms_norm(jnp.concatenate([k_nope, k_rope], axis=-1), k_gain)
    if rope is not None:
        q = jnp.concatenate([q[..., :MLA_NOPE], apply_rope(q[..., MLA_NOPE:], rope)], axis=-1)
        k = jnp.concatenate([k[..., :MLA_NOPE], apply_rope(k[..., MLA_NOPE:], rope)], axis=-1)
    return q, k, v


def block_attention(q, k, v):
    b, h, t_len, d = q.shape
    nb = t_len // Q_BLOCK
    qb = jnp.moveaxis(q.reshape(b, h, nb, Q_BLOCK, d), 2, 0)
    scale = d ** -0.5

    def one_block(qi):
        s = jnp.einsum('bhqd,bhkd->bhqk', qi, k).astype(jnp.float32) * scale
        p = jax.nn.softmax(s, axis=-1).astype(v.dtype)
        return jnp.einsum('bhqk,bhkd->bhqd', p, v)

    o = lax.map(one_block, qb)
    return jnp.moveaxis(o, 0, 2).reshape(b, h, t_len, v.shape[-1])


def mla_mixer(h_lat, h_ctx, w_down, q_norm, kv_norm, w_uq, w_ukv, q_gain, k_gain, w_out, rope, need_ctx):
    ql, kl, vl = mla_project(h_lat, w_down, q_norm, kv_norm, w_uq, w_ukv, q_gain, k_gain, rope)
    qc, kc, vc = mla_project(h_ctx, w_down, q_norm, kv_norm, w_uq, w_ukv, q_gain, k_gain, None)
    k_all = jnp.concatenate([kl, kc], axis=2)
    v_all = jnp.concatenate([vl, vc], axis=2)
    y_lat = merge_heads(block_attention(ql, k_all, v_all)) @ w_out
    y_ctx = merge_heads(block_attention(qc, kc, vc)) @ w_out if need_ctx else None
    return y_lat, y_ctx


def conv_ffn(h, w_in, conv_w, conv_b, w_out):
    gate, up = jnp.split(h @ w_in, 2, axis=-1)
    gate = lax.conv_general_dilated(
        gate, conv_w[:, None, :], window_strides=(1,), padding=((CONV_W // 2, CONV_W // 2),),
        dimension_numbers=('NWC', 'WIO', 'NWC'), feature_group_count=D_FF) + conv_b
    return (jax.nn.gelu(gate) * up) @ w_out


def setup_inputs(seed: int = 0) -> dict:
    key = jax.random.key(seed)
    keys = iter(jax.random.split(key, 32))

    def normal(shape, scale=1.0):
        return scale * jax.random.normal(next(keys), shape, jnp.float32)

    def gain(shape):
        return 1.0 + normal(shape, 0.02)

    d = D_MODEL
    return {
        'x': normal((BATCH, SEQ, d)),
        'c': normal((BATCH, d)),
        'ctx': normal((BATCH, CTX_LEN, d)),
        'c_ctx': normal((d,)),
        'ada_down': normal((DEPTH, d, ADA_RANK), d ** -0.5),
        'ada_up': normal((DEPTH, ADA_RANK, 6 * d), 0.5 * ADA_RANK ** -0.5),
        'gla_w_in': normal((N_GLA, d, GLA_IN), d ** -0.5),
        'gla_w_a2': normal((N_GLA, 2, GLA_GATE_RANK, GLA_DK), GLA_GATE_RANK ** -0.5),
        'gla_b_a': normal((N_GLA, 2, GLA_DK), 0.1),
        'gla_b_r': normal((N_GLA, GLA_DV), 0.02),
        'gla_norm': gain((N_GLA, GLA_DV_HEAD)),
        'gla_w_out': normal((N_GLA, GLA_DV, d), GLA_DV ** -0.5),
        'ret_w_in': normal((N_RET, d, RET_IN), d ** -0.5),
        'ret_decay': 5.0 + jnp.arange(RET_HEADS, dtype=jnp.float32) + normal((N_RET, 2, RET_HEADS), 0.1),
        'ret_gn_w': gain((N_RET, RET_DV)),
        'ret_gn_b': normal((N_RET, RET_DV), 0.02),
        'ret_w_out': normal((N_RET, RET_DV, d), RET_DV ** -0.5),
        'mla_w_down': normal((N_MLA, d, MLA_DOWN), d ** -0.5),
        'mla_q_norm': gain((N_MLA, MLA_Q_RANK)),
        'mla_kv_norm': gain((N_MLA, MLA_KV_RANK)),
        'mla_w_uq': normal((N_MLA, MLA_Q_RANK, MLA_HEADS * MLA_QK_DIM), MLA_Q_RANK ** -0.5),
        'mla_w_ukv': normal((N_MLA, MLA_KV_RANK, MLA_HEADS * (MLA_NOPE + MLA_V)), MLA_KV_RANK ** -0.5),
        'mla_q_gain': gain((N_MLA, MLA_QK_DIM)),
        'mla_k_gain': gain((N_MLA, MLA_QK_DIM)),
        'mla_w_out': normal((N_MLA, MLA_HEADS * MLA_V, d), (MLA_HEADS * MLA_V) ** -0.5),
        'ffn_w_in': normal((DEPTH, d, 2 * D_FF), d ** -0.5),
        'ffn_conv_w': normal((DEPTH, CONV_W, D_FF), CONV_W ** -0.5),
        'ffn_conv_b': normal((DEPTH, D_FF), 0.02),
        'ffn_w_out': normal((DEPTH, D_FF, d), D_FF ** -0.5),
    }


def reference(x, c, ctx, c_ctx, ada_down, ada_up,
              gla_w_in, gla_w_a2, gla_b_a, gla_b_r, gla_norm, gla_w_out,
              ret_w_in, ret_decay, ret_gn_w, ret_gn_b, ret_w_out,
              mla_w_down, mla_q_norm, mla_kv_norm, mla_w_uq, mla_w_ukv, mla_q_gain, mla_k_gain, mla_w_out,
              ffn_w_in, ffn_conv_w, ffn_conv_b, ffn_w_out):
    n_tokens = x.shape[1]
    ret_rope = axial_rope(n_tokens, RET_DK_HEAD)
    mla_rope = axial_rope(n_tokens, MLA_ROPE)
    c_ctx_b = c_ctx[None, :]
    for i in range(DEPTH):
        kind, j = i % N_MIXERS, i // N_MIXERS
        need_ctx = i < DEPTH - 1
        sh1, sc1, g1, sh2, sc2, g2 = ada_modulation(c, ada_down[i], ada_up[i])
        csh1, csc1, cg1, csh2, csc2, cg2 = ada_modulation(c_ctx_b, ada_down[i], ada_up[i])
        h_lat = modulate(x, sh1, sc1)
        h_ctx = modulate(ctx, csh1, csc1)
        if kind == 0:
            y_lat, y_ctx = gla_mixer(h_lat, h_ctx, gla_w_in[j], gla_w_a2[j], gla_b_a[j], gla_b_r[j],
                                     gla_norm[j], gla_w_out[j], need_ctx)
        elif kind == 1:
            y_lat, y_ctx = retention_mixer(h_lat, h_ctx, ret_w_in[j], ret_decay[j], ret_gn_w[j], ret_gn_b[j],
                                           ret_w_out[j], ret_rope, need_ctx)
        else:
            y_lat, y_ctx = mla_mixer(h_lat, h_ctx, mla_w_down[j], mla_q_norm[j], mla_kv_norm[j], mla_w_uq[j],
                                     mla_w_ukv[j], mla_q_gain[j], mla_k_gain[j], mla_w_out[j], mla_rope, need_ctx)
        x = x + g1 * y_lat
        x = x + g2 * conv_ffn(modulate(x, sh2, sc2), ffn_w_in[i], ffn_conv_w[i], ffn_conv_b[i], ffn_w_out[i])
        if need_ctx:
            ctx = ctx + cg1 * y_ctx
            ctx = ctx + cg2 * conv_ffn(modulate(ctx, csh2, csc2), ffn_w_in[i], ffn_conv_w[i], ffn_conv_b[i], ffn_w_out[i])
    return x
```

```python
import functools
import math

import jax
import jax.numpy as jnp
from jax import lax
from jax.experimental import pallas as pl
from jax.experimental.pallas import tpu as pltpu

GRID_W = 64
EPS = 1e-6
ROPE_BASE = 10000.0
GLA_HEADS = 8
GLA_TAU = 16.0
MLA_NOPE = 128
MLA_ROPE = 64

LANES = 128
SUBLANES_BF16 = 16
VMEM_LIMIT_BYTES = 56 * 1024 * 1024

ROW_TILE = 256
GLA_CHUNK = 64
GLA_SUB = 16
SEQ_BLOCK = 256

_HIGHEST = lax.Precision.HIGHEST
_F32 = jnp.float32
_BF16 = jnp.bfloat16


def _pick_tile(n, cap, align):
    best = None
    d = align
    while d <= min(n, cap):
        if n % d == 0:
            best = d
        d += align
    return best if best is not None else n


def _params(sem, vmem=None):
    return pltpu.CompilerParams(dimension_semantics=sem, vmem_limit_bytes=vmem or VMEM_LIMIT_BYTES)


def _silu(x):
    return x * (1.0 / (1.0 + jnp.exp(-x)))


def _mm_fullk_kernel(a_ref, b_ref, o_ref):
    o_ref[...] = jnp.dot(a_ref[...], b_ref[...], preferred_element_type=_F32).astype(o_ref.dtype)


def _mm_ksplit_kernel(a_ref, b_ref, o_ref, acc_ref, *, nk):
    k = pl.program_id(2)

    @pl.when(k == 0)
    def _():
        acc_ref[...] = jnp.zeros_like(acc_ref)

    acc_ref[...] += jnp.dot(a_ref[...], b_ref[...], preferred_element_type=_F32)

    @pl.when(k == nk - 1)
    def _():
        o_ref[...] = acc_ref[...].astype(o_ref.dtype)


def mm(a, b, out_dtype, m_rows=None, name="mm"):
    m = a.shape[0] if m_rows is None else m_rows
    k, n = b.shape
    assert a.shape[1] == k and a.dtype == _BF16 and b.dtype == _BF16
    tm = _pick_tile(m, 1056, SUBLANES_BF16)
    tn = _pick_tile(n, 1024, LANES)
    if n > 1024 and tn < 512:
        tn = n
    tk = k if k <= 4096 else _pick_tile(k, 2816, LANES)
    out_shape = jax.ShapeDtypeStruct((m, n), out_dtype)
    if tk == k:
        return pl.pallas_call(
            _mm_fullk_kernel,
            grid=(m // tm, n // tn),
            in_specs=[pl.BlockSpec((tm, k), lambda i, j: (i, 0)),
                      pl.BlockSpec((k, tn), lambda i, j: (0, j))],
            out_specs=pl.BlockSpec((tm, tn), lambda i, j: (i, j)),
            out_shape=out_shape,
            compiler_params=_params(("parallel", "arbitrary")),
            name=name,
        )(a, b)
    nk = k // tk
    return pl.pallas_call(
        functools.partial(_mm_ksplit_kernel, nk=nk),
        grid=(m // tm, n // tn, nk),
        in_specs=[pl.BlockSpec((tm, tk), lambda i, j, kk: (i, kk)),
                  pl.BlockSpec((tk, tn), lambda i, j, kk: (kk, j))],
        out_specs=pl.BlockSpec((tm, tn), lambda i, j, kk: (i, j)),
        out_shape=out_shape,
        scratch_shapes=[pltpu.VMEM((tm, tn), _F32)],
        compiler_params=_params(("parallel", "arbitrary", "arbitrary")),
        name=name,
    )(a, b)


def _ada_kernel(cond_ref, down_ref, up_ref, o_ref, t_ref):
    @pl.when(pl.program_id(1) == 0)
    def _():
        t_ref[...] = jnp.dot(_silu(cond_ref[...]), down_ref[...], precision=_HIGHEST,
                             preferred_element_type=_F32)

    o_ref[...] = jnp.dot(t_ref[...], up_ref[...], precision=_HIGHEST, preferred_element_type=_F32)


def ada_all(cond8, ada_down, ada_up):
    depth, d, r = ada_down.shape
    n = ada_up.shape[-1]
    tn = _pick_tile(n, 2048, LANES)
    return pl.pallas_call(
        _ada_kernel,
        grid=(depth, n // tn),
        in_specs=[pl.BlockSpec((8, d), lambda l, j: (0, 0)),
                  pl.BlockSpec((None, d, r), lambda l, j: (l, 0, 0)),
                  pl.BlockSpec((None, r, tn), lambda l, j: (l, 0, j))],
        out_specs=pl.BlockSpec((None, 8, tn), lambda l, j: (l, 0, j)),
        out_shape=jax.ShapeDtypeStruct((depth, 8, n), _F32),
        scratch_shapes=[pltpu.VMEM((8, r), _F32)],
        compiler_params=_params(("arbitrary", "arbitrary")),
        name="ada",
    )(cond8, ada_down, ada_up)


def _resmod_kernel(*refs, has_y, has_mod):
    it = iter(refs)
    x_ref = next(it)
    if has_y:
        y_ref, g_ref = next(it), next(it)
    if has_mod:
        sh_ref, sc_ref = next(it), next(it)
    if has_y:
        xo_ref = next(it)
    if has_mod:
        h_ref = next(it)
    x = x_ref[...]
    if has_y:
        x = x + g_ref[...] * y_ref[...]
        xo_ref[...] = x
    if has_mod:
        ms = jnp.mean(x * x, axis=-1, keepdims=True)
        h = x * lax.rsqrt(ms + EPS)
        h_ref[...] = (h * (1.0 + sc_ref[...]) + sh_ref[...]).astype(h_ref.dtype)


def resmod(xa, n_lat, y=None, gate=None, shift=None, scale=None, m_rows=None):
    m = xa.shape[0] if m_rows is None else m_rows
    d = xa.shape[1]
    tr = ROW_TILE
    assert m % tr == 0 and n_lat % tr == 0
    has_y, has_mod = y is not None, shift is not None
    n_lat_blocks = n_lat // tr
    row = pl.BlockSpec((tr, d), lambda i: (i, 0))
    vec = pl.BlockSpec((None, 1, d), lambda i: (jnp.where(i >= n_lat_blocks, 1, 0), 0, 0))
    args, in_specs, out_shape, out_specs = [xa], [row], [], []
    if has_y:
        args += [y, gate]
        in_specs += [row, vec]
        out_shape.append(jax.ShapeDtypeStruct((m, d), _F32))
        out_specs.append(row)
    if has_mod:
        args += [shift, scale]
        in_specs += [vec, vec]
        out_shape.append(jax.ShapeDtypeStruct((m, d), _BF16))
        out_specs.append(row)
    outs = pl.pallas_call(
        functools.partial(_resmod_kernel, has_y=has_y, has_mod=has_mod),
        grid=(m // tr,),
        in_specs=in_specs,
        out_specs=out_specs,
        out_shape=out_shape,
        compiler_params=_params(("parallel",)),
        name="resmod",
    )(*args)
    return outs


def _convglu_kernel(g_ref, u_ref, gp_ref, gn_ref, w_ref, b_ref, o_ref, *, tr, n_lat_blocks, n_blocks):
    i = pl.program_id(0)
    g = g_ref[...].astype(_F32)
    hs = SUBLANES_BF16
    starts_seq = jnp.logical_or(i == 0, i == n_lat_blocks)
    ends_seq = jnp.logical_or(i == n_lat_blocks - 1, i == n_blocks - 1)
    prev_row = jnp.where(starts_seq, 0.0, gp_ref[hs - 1:hs, :].astype(_F32))
    next_row = jnp.where(ends_seq, 0.0, gn_ref[0:1, :].astype(_F32))
    rows = lax.broadcasted_iota(jnp.int32, g.shape, 0)
    g_prev = jnp.where(rows == 0, prev_row, pltpu.roll(g, 1, axis=0))
    g_next = jnp.where(rows == tr - 1, next_row, pltpu.roll(g, tr - 1, axis=0))
    w = w_ref[...]
    z = g_prev * w[0:1, :] + g * w[1:2, :] + g_next * w[2:3, :] + b_ref[...]
    c0 = math.sqrt(2.0 / math.pi)
    gelu = 0.5 * z * (1.0 + jnp.tanh(c0 * (z + 0.044715 * (z * z * z))))
    o_ref[...] = (gelu * u_ref[...].astype(_F32)).astype(o_ref.dtype)


def convglu(gu, conv_w, conv_b, n_lat, f_pad):
    m = gu.shape[0]
    tr = ROW_TILE
    tc = _pick_tile(f_pad, 1408, LANES)
    hs = SUBLANES_BF16
    ncb = f_pad // tc
    n_blocks = m // tr
    last_halo = m // hs - 1
    return pl.pallas_call(
        functools.partial(_convglu_kernel, tr=tr, n_lat_blocks=n_lat // tr, n_blocks=n_blocks),
        grid=(n_blocks, ncb),
        in_specs=[pl.BlockSpec((tr, tc), lambda i, j: (i, j)),
                  pl.BlockSpec((tr, tc), lambda i, j: (i, j + ncb)),
                  pl.BlockSpec((hs, tc), lambda i, j: (jnp.maximum(i * (tr // hs) - 1, 0), j)),
                  pl.BlockSpec((hs, tc), lambda i, j: (jnp.minimum((i + 1) * (tr // hs), last_halo), j)),
                  pl.BlockSpec((8, tc), lambda i, j: (0, j)),
                  pl.BlockSpec((1, tc), lambda i, j: (0, j))],
        out_specs=pl.BlockSpec((tr, tc), lambda i, j: (i, j)),
        out_shape=jax.ShapeDtypeStruct((m, f_pad), _BF16),
        compiler_params=_params(("parallel", "arbitrary")),
        name="convglu",
    )(gu, gu, gu, gu, conv_w, conv_b)


def _seq_block_index(c, n_lat_blocks, n_ctx_blocks, reverse):
    if reverse:
        ctx = n_lat_blocks + n_ctx_blocks - 1 - c
        lat = n_lat_blocks - 1 - (c - n_ctx_blocks)
    else:
        ctx = n_lat_blocks + c
        lat = c - n_ctx_blocks
    return jnp.where(c < n_ctx_blocks, ctx, lat)


def _gla_kernel(q_ref, k_ref, v_ref, a_ref, w2_ref, ba_ref, o_ref, s_ref, *, reverse, rank, scale):
    C, SUB = GLA_CHUNK, GLA_SUB
    n_chunks = SEQ_BLOCK // C
    n_sub = C // SUB
    dk = q_ref.shape[1]
    dv = v_ref.shape[1]

    @pl.when(pl.program_id(1) == 0)
    def _():
        s_ref[...] = jnp.zeros_like(s_ref)

    t_idx = lax.broadcasted_iota(jnp.int32, (C, C), 0)
    u_idx = lax.broadcasted_iota(jnp.int32, (C, C), 1)
    tri = jnp.where((u_idx >= t_idx) if reverse else (u_idx <= t_idx), 1.0, 0.0).astype(_F32)
    ones_cols = jnp.ones((C, LANES), _F32)
    key_idx = lax.broadcasted_iota(jnp.int32, (SUB, C), 1)
    sub_rows = lax.broadcasted_iota(jnp.int32, (SUB, 1), 0)
    w2 = w2_ref[...]
    ba = ba_ref[...]
    lane0 = rank if reverse else 0

    def chunk(ci, carry):
        cidx = (n_chunks - 1 - ci) if reverse else ci
        r0 = pl.multiple_of(cidx * C, C)
        rows = pl.ds(r0, C)
        a = a_ref[rows, :][:, lane0:lane0 + rank]
        z = jnp.dot(a, w2, precision=_HIGHEST, preferred_element_type=_F32) + ba
        la = (jnp.minimum(z, 0.0) - jnp.log1p(jnp.exp(-jnp.abs(z)))) * (1.0 / GLA_TAU)
        b = jnp.dot(tri, la, precision=_HIGHEST, preferred_element_type=_F32)
        q = q_ref[rows, :] * scale
        k = k_ref[rows, :]
        v = v_ref[rows, :]
        s = s_ref[...]
        o = jnp.dot((q * jnp.exp(b)).astype(_BF16), s.astype(_BF16), preferred_element_type=_F32)

        a_rows = []
        for i in range(n_sub):
            lo = i * SUB
            bi, qi, ki = b[lo:lo + SUB], q[lo:lo + SUB], k[lo:lo + SUB]
            has_earlier = (i < n_sub - 1) if reverse else (i > 0)
            if has_earlier:
                ref_row = lo + SUB - 1 if reverse else lo
                bref = b[ref_row:ref_row + 1]
                qe = qi * jnp.exp(bi - bref)
                ke = k * jnp.exp(jnp.minimum(bref - b, 0.0))
                off = lax.dot_general(qe.astype(_BF16), ke.astype(_BF16), (((1,), (1,)), ((), ())),
                                      preferred_element_type=_F32)
                earlier = (key_idx >= lo + SUB) if reverse else (key_idx < lo)
                ai = jnp.where(earlier, off, 0.0)
            else:
                ai = jnp.zeros((SUB, C), _F32)
            for s_i in range(SUB):
                seen = (sub_rows <= s_i) if reverse else (sub_rows >= s_i)
                dlt = jnp.where(seen, bi - bi[s_i:s_i + 1], -jnp.inf)
                wgt = qi * ki[s_i:s_i + 1] * jnp.exp(dlt)
                col = jnp.sum(wgt, axis=1, keepdims=True)
                ai = jnp.where(key_idx == lo + s_i, col, ai)
            a_rows.append(ai)
        amat = jnp.concatenate(a_rows, axis=0)
        o = o + jnp.dot(amat.astype(_BF16), v, preferred_element_type=_F32)
        o_ref[rows, :] = o

        end_row = 0 if reverse else C - 1
        bend = b[end_row:end_row + 1]
        kd = (k * jnp.exp(bend - b)).astype(_BF16)
        tot = lax.dot_general(la, ones_cols, (((0,), (0,)), ((), ())), precision=_HIGHEST,
                              preferred_element_type=_F32)
        dec = jnp.exp(tot)
        dec = jnp.concatenate([dec] * (dv // LANES), axis=1)
        s_ref[...] = dec * s + lax.dot_general(kd, v, (((0,), (0,)), ((), ())), preferred_element_type=_F32)
        return carry

    lax.fori_loop(0, n_chunks, chunk, 0)


def gla_core(qk, v, a, w_a2, b_a, n_lat, reverse):
    m = qk.shape[0]
    dk_all = qk.shape[1] // 2
    dv_all = v.shape[1]
    heads = GLA_HEADS
    dkh, dvh = dk_all // heads, dv_all // heads
    rank = w_a2.shape[1]
    tb = SEQ_BLOCK
    nlb, ncb = n_lat // tb, (m - n_lat) // tb
    d = 1 if reverse else 0
    blk = functools.partial(_seq_block_index, n_lat_blocks=nlb, n_ctx_blocks=ncb, reverse=reverse)
    return pl.pallas_call(
        functools.partial(_gla_kernel, reverse=reverse, rank=rank, scale=dkh ** -0.5),
        grid=(heads, nlb + ncb),
        in_specs=[pl.BlockSpec((tb, dkh), lambda h, c: (blk(c), h)),
                  pl.BlockSpec((tb, dkh), lambda h, c: (blk(c), heads + h)),
                  pl.BlockSpec((tb, dvh), lambda h, c: (blk(c), h)),
                  pl.BlockSpec((tb, LANES), lambda h, c: (blk(c), 0)),
                  pl.BlockSpec((None, rank, dkh), lambda h, c: (d, 0, h)),
                  pl.BlockSpec((None, 1, dkh), lambda h, c: (d, 0, h))],
        out_specs=pl.BlockSpec((tb, dvh), lambda h, c: (blk(c), h)),
        out_shape=jax.ShapeDtypeStruct((m, dv_all), _F32),
        scratch_shapes=[pltpu.VMEM((dkh, dvh), _F32)],
        compiler_params=_params(("parallel", "arbitrary")),
        name="gla_rev" if reverse else "gla_fwd",
    )(qk, qk, v, a, w_a2, b_a)


def _gla_gate_kernel(of_ref, ob_ref, r_ref, br_ref, g_ref, o_ref):
    o = of_ref[...] + ob_ref[...]
    o = o * lax.rsqrt(jnp.mean(o * o, axis=-1, keepdims=True) + EPS) * g_ref[...]
    o_ref[...] = (_silu(r_ref[...] + br_ref[...]) * o).astype(o_ref.dtype)


def gla_gate(o_f, o_b, r, b_r, norm_g, m_rows):
    dv_all = o_f.shape[1]
    dvh = norm_g.shape[-1]
    tr = ROW_TILE
    blk = pl.BlockSpec((tr, dvh), lambda i, h: (i, h))
    return pl.pallas_call(
        _gla_gate_kernel,
        grid=(m_rows // tr, dv_all // dvh),
        in_specs=[blk, blk, blk,
                  pl.BlockSpec((1, dvh), lambda i, h: (0, h)),
                  pl.BlockSpec((1, dvh), lambda i, h: (0, 0))],
        out_specs=blk,
        out_shape=jax.ShapeDtypeStruct((m_rows, dv_all), _BF16),
        compiler_params=_params(("parallel", "arbitrary")),
        name="gla_gate",
    )(o_f, o_b, r, b_r, norm_g)


def _ret_kernel(q_ref, k_ref, v_ref, cos_ref, sin_ref, e_ref, o_ref, s_ref, *, reverse, scale):
    C = SEQ_BLOCK
    dk = q_ref.shape[1]
    dv = v_ref.shape[1]
    half = dk // 2

    @pl.when(pl.program_id(1) == 0)
    def _():
        s_ref[...] = jnp.zeros_like(s_ref)

    lg = jnp.log1p(-jnp.exp2(-e_ref[...]))[0:1, 0:1]
    cos, sin = cos_ref[...], sin_ref[...]

    def rope(x):
        x1, x2 = x[:, :half], x[:, half:]
        return jnp.concatenate([x1 * cos - x2 * sin, x1 * sin + x2 * cos], axis=1)

    q = rope(q_ref[...] * scale)
    k = rope(k_ref[...])
    v = v_ref[...]
    t_idx = lax.broadcasted_iota(jnp.int32, (C, C), 0).astype(_F32)
    s_idx = lax.broadcasted_iota(jnp.int32, (C, C), 1).astype(_F32)
    rel = (s_idx - t_idx) if reverse else (t_idx - s_idx)
    decay = jnp.exp(jnp.where(rel >= 0, lg * rel, -jnp.inf))
    pos = lax.broadcasted_iota(jnp.int32, (C, dk), 0).astype(_F32)
    if reverse:
        q_dec, k_dec = jnp.exp(lg * (C - pos)), jnp.exp(lg * pos)
    else:
        q_dec, k_dec = jnp.exp(lg * (pos + 1.0)), jnp.exp(lg * (C - 1.0 - pos))
    s = s_ref[...]
    scores = lax.dot_general(q.astype(_BF16), k.astype(_BF16), (((1,), (1,)), ((), ())),
                             preferred_element_type=_F32) * decay
    o = jnp.dot(scores.astype(_BF16), v, preferred_element_type=_F32)
    o = o + jnp.dot((q * q_dec).astype(_BF16), s.astype(_BF16), preferred_element_type=_F32)
    o_ref[...] = o
    state_dec = jnp.exp(lg * float(C))
    s_ref[...] = state_dec * s + lax.dot_general((k * k_dec).astype(_BF16), v, (((0,), (0,)), ((), ())),
                                                 preferred_element_type=_F32)


def ret_core(qk, v, cos, sin, dec_b, n_lat, heads, reverse):
    m = qk.shape[0]
    dkh = qk.shape[1] // 2 // heads
    dvh = v.shape[1] // heads
    tb = SEQ_BLOCK
    nlb, ncb = n_lat // tb, (m - n_lat) // tb
    d = 1 if reverse else 0
    blk = functools.partial(_seq_block_index, n_lat_blocks=nlb, n_ctx_blocks=ncb, reverse=reverse)
    return pl.pallas_call(
        functools.partial(_ret_kernel, reverse=reverse, scale=dkh ** -0.5),
        grid=(heads, nlb + ncb),
        in_specs=[pl.BlockSpec((tb, dkh), lambda h, c: (blk(c), h)),
                  pl.BlockSpec((tb, dkh), lambda h, c: (blk(c), heads + h)),
                  pl.BlockSpec((tb, dvh), lambda h, c: (blk(c), h)),
                  pl.BlockSpec((tb, dkh // 2), lambda h, c: (blk(c), 0)),
                  pl.BlockSpec((tb, dkh // 2), lambda h, c: (blk(c), 0)),
                  pl.BlockSpec((None, None, 8, LANES), lambda h, c: (d, h, 0, 0))],
        out_specs=pl.BlockSpec((tb, dvh), lambda h, c: (blk(c), h)),
        out_shape=jax.ShapeDtypeStruct((m, v.shape[1]), _F32),
        scratch_shapes=[pltpu.VMEM((dkh, dvh), _F32)],
        compiler_params=_params(("parallel", "arbitrary")),
        name="ret_rev" if reverse else "ret_fwd",
    )(qk, qk, v, cos, sin, dec_b)


def _ret_gate_kernel(of_ref, ob_ref, g_ref, w_ref, b_ref, o_ref):
    o = of_ref[...] + ob_ref[...]
    mu = jnp.mean(o, axis=-1, keepdims=True)
    oc = o - mu
    var = jnp.mean(oc * oc, axis=-1, keepdims=True)
    o = oc * lax.rsqrt(var + EPS) * w_ref[...] + b_ref[...]
    o_ref[...] = (_silu(g_ref[...]) * o).astype(o_ref.dtype)


def ret_gate(o_f, o_b, g, gn_w, gn_b, heads, m_rows):
    dv_all = o_f.shape[1]
    dvh = dv_all // heads
    tr = ROW_TILE
    blk = pl.BlockSpec((tr, dvh), lambda i, h: (i, h))
    vec = pl.BlockSpec((1, dvh), lambda i, h: (0, h))
    return pl.pallas_call(
        _ret_gate_kernel,
        grid=(m_rows // tr, heads),
        in_specs=[blk, blk, blk, vec, vec],
        out_specs=blk,
        out_shape=jax.ShapeDtypeStruct((m_rows, dv_all), _BF16),
        compiler_params=_params(("parallel", "arbitrary")),
        name="ret_gate",
    )(o_f, o_b, g, gn_w, gn_b)


def _mla_norm_kernel(cq_ref, ckv_ref, qn_ref, kvn_ref, oq_ref, okv_ref):
    cq = cq_ref[...]
    oq_ref[...] = (cq * lax.rsqrt(jnp.mean(cq * cq, axis=-1, keepdims=True) + EPS) * qn_ref[...]).astype(oq_ref.dtype)
    ckv = ckv_ref[...]
    okv_ref[...] = (ckv * lax.rsqrt(jnp.mean(ckv * ckv, axis=-1, keepdims=True) + EPS)
                    * kvn_ref[...]).astype(okv_ref.dtype)


def mla_norm(down, q_norm, kv_norm):
    m = down.shape[0]
    qr, kvr = q_norm.shape[-1], kv_norm.shape[-1]
    assert qr % kvr == 0
    tr = ROW_TILE
    return pl.pallas_call(
        _mla_norm_kernel,
        grid=(m // tr,),
        in_specs=[pl.BlockSpec((tr, qr), lambda i: (i, 0)),
                  pl.BlockSpec((tr, kvr), lambda i: (i, qr // kvr)),
                  pl.BlockSpec((1, qr), lambda i: (0, 0)),
                  pl.BlockSpec((1, kvr), lambda i: (0, 0))],
        out_specs=[pl.BlockSpec((tr, qr), lambda i: (i, 0)), pl.BlockSpec((tr, kvr), lambda i: (i, 0))],
        out_shape=[jax.ShapeDtypeStruct((m, qr), _BF16), jax.ShapeDtypeStruct((m, kvr), _BF16)],
        compiler_params=_params(("parallel",)),
        name="mla_norm",
    )(down, down, q_norm, kv_norm)


def _swap_rope_halves(x):
    lane = lax.broadcasted_iota(jnp.int32, x.shape, 1)
    return jnp.where((lane & 32) == 0, pltpu.roll(x, LANES - 32, axis=1), pltpu.roll(x, 32, axis=1))


def _mla_prep_kernel(qn_ref, qr_ref, kn_ref, kr_ref, c_ref, s_ref, gqn_ref, gqr_ref, gkn_ref, gkr_ref,
                     oq_ref, ok_ref, *, qk_dim, scale):
    cosv, sinv = c_ref[...], s_ref[...]
    lane = lax.broadcasted_iota(jnp.int32, cosv.shape, 1)
    low = lane < MLA_ROPE
    qr = qr_ref[...]
    kr = kr_ref[...]
    kr_sq = jnp.sum(kr * kr, axis=-1, keepdims=True)
    kr_g = kr * gkr_ref[...]
    kr_rot = kr_g * cosv + _swap_rope_halves(kr_g) * sinv
    for hh in range(2):
        qn = qn_ref[:, hh * MLA_NOPE:(hh + 1) * MLA_NOPE]
        mine = low if hh == 0 else jnp.logical_not(low)
        q_sq = jnp.sum(qn * qn, axis=-1, keepdims=True) + jnp.sum(jnp.where(mine, qr * qr, 0.0), axis=-1, keepdims=True)
        q_rr = lax.rsqrt(q_sq * (1.0 / qk_dim) + EPS) * scale
        qr_g = qr * q_rr * gqr_ref[...]
        qr_rot = qr_g * cosv + _swap_rope_halves(qr_g) * sinv
        if hh == 1:
            qr_rot = pltpu.roll(qr_rot, MLA_ROPE, axis=1)
        oq_ref[hh, :, 0:MLA_NOPE] = (qn * q_rr * gqn_ref[...]).astype(oq_ref.dtype)
        oq_ref[hh, :, MLA_NOPE:2 * MLA_NOPE] = jnp.where(low, qr_rot, 0.0).astype(oq_ref.dtype)

        kn = kn_ref[:, hh * MLA_NOPE:(hh + 1) * MLA_NOPE]
        k_sq = jnp.sum(kn * kn, axis=-1, keepdims=True) + kr_sq
        k_rr = lax.rsqrt(k_sq * (1.0 / qk_dim) + EPS)
        ok_ref[hh, :, 0:MLA_NOPE] = (kn * k_rr * gkn_ref[...]).astype(ok_ref.dtype)
        ok_ref[hh, :, MLA_NOPE:2 * MLA_NOPE] = jnp.where(low, kr_rot * k_rr, 0.0).astype(ok_ref.dtype)


def mla_prep(qraw, knope, down, kr_block, cos2, sin2, q_gain, k_gain, heads):
    m = qraw.shape[0]
    qk_dim = MLA_NOPE + MLA_ROPE
    tr = _pick_tile(m, 1056, SUBLANES_BF16)
    nope_blocks = heads * MLA_NOPE // (2 * MLA_NOPE)
    gqn, gkn = q_gain[None, :MLA_NOPE], k_gain[None, :MLA_NOPE]
    gqr = jnp.tile(q_gain[MLA_NOPE:], 2)[None]
    gkr = jnp.tile(k_gain[MLA_NOPE:], 2)[None]
    vec = pl.BlockSpec((1, LANES), lambda i, p: (0, 0))
    out = pl.BlockSpec((2, tr, 2 * MLA_NOPE), lambda i, p: (p, i, 0))
    return pl.pallas_call(
        functools.partial(_mla_prep_kernel, qk_dim=qk_dim, scale=qk_dim ** -0.5),
        grid=(m // tr, heads // 2),
        in_specs=[pl.BlockSpec((tr, 2 * MLA_NOPE), lambda i, p: (i, p)),
                  pl.BlockSpec((tr, LANES), lambda i, p: (i, 2 * nope_blocks + p)),
                  pl.BlockSpec((tr, 2 * MLA_NOPE), lambda i, p: (i, p)),
                  pl.BlockSpec((tr, LANES), lambda i, p: (i, kr_block)),
                  pl.BlockSpec((tr, LANES), lambda i, p: (i, 0)),
                  pl.BlockSpec((tr, LANES), lambda i, p: (i, 0)),
                  vec, vec, vec, vec],
        out_specs=[out, out],
        out_shape=[jax.ShapeDtypeStruct((heads, m, 2 * MLA_NOPE), _BF16)] * 2,
        compiler_params=_params(("parallel", "arbitrary")),
        name="mla_prep",
    )(qraw, qraw, knope, down, cos2, sin2, gqn, gqr, gkn, gkr)


def _attn_kernel(q_ref, k_ref, v_ref, o_ref, m_ref, l_ref, acc_ref, *, tkv, n_kv):
    q = q_ref[...]
    m_ref[...] = jnp.full_like(m_ref, -jnp.inf)
    l_ref[...] = jnp.zeros_like(l_ref)
    acc_ref[...] = jnp.zeros_like(acc_ref)

    def body(j, carry):
        rows = pl.ds(pl.multiple_of(j * tkv, tkv), tkv)
        s = lax.dot_general(q, k_ref[rows, :], (((1,), (1,)), ((), ())), preferred_element_type=_F32)
        m_prev = m_ref[...]
        m_new = jnp.maximum(m_prev, jnp.max(s, axis=-1, keepdims=True))
        alpha = jnp.exp(m_prev - m_new)
        p = jnp.exp(s - m_new)
        l_ref[...] = alpha * l_ref[...] + jnp.sum(p, axis=-1, keepdims=True)
        acc_ref[...] = alpha * acc_ref[...] + jnp.dot(p.astype(_BF16), v_ref[rows, :], preferred_element_type=_F32)
        m_ref[...] = m_new
        return carry

    lax.fori_loop(0, n_kv, body, 0)
    o_ref[...] = (acc_ref[...] / l_ref[...]).astype(o_ref.dtype)


def attention(q_hm, k_hm, v, q_row0, n_q, kv_row0, n_kv_rows):
    heads, _, dqk = q_hm.shape
    dv = v.shape[1] // heads
    tq = _pick_tile(n_q, 512, SUBLANES_BF16)
    tkv = _pick_tile(n_kv_rows, 768, 256)
    assert q_row0 % tq == 0 and kv_row0 % n_kv_rows == 0
    q0, kv0 = q_row0 // tq, kv_row0 // n_kv_rows
    return pl.pallas_call(
        functools.partial(_attn_kernel, tkv=tkv, n_kv=n_kv_rows // tkv),
        grid=(heads, n_q // tq),
        in_specs=[pl.BlockSpec((None, tq, dqk), lambda h, i: (h, q0 + i, 0)),
                  pl.BlockSpec((None, n_kv_rows, dqk), lambda h, i: (h, kv0, 0)),
                  pl.BlockSpec((n_kv_rows, dv), lambda h, i: (kv0, h))],
        out_specs=pl.BlockSpec((tq, dv), lambda h, i: (i, h)),
        out_shape=jax.ShapeDtypeStruct((n_q, heads * dv), _BF16),
        scratch_shapes=[pltpu.VMEM((tq, 1), _F32), pltpu.VMEM((tq, 1), _F32), pltpu.VMEM((tq, dv), _F32)],
        compiler_params=_params(("parallel", "arbitrary")),
        name="attn",
    )(q_hm, k_hm, v)


def _axial_rope(n_tokens, dim):
    rows = n_tokens // GRID_W
    quarter = dim // 4
    inv_freq = ROPE_BASE ** (-jnp.arange(quarter, dtype=_F32) / quarter)
    row = jnp.repeat(jnp.arange(rows, dtype=_F32), GRID_W)
    col = jnp.tile(jnp.arange(GRID_W, dtype=_F32), rows)
    ang = jnp.concatenate([row[:, None] * inv_freq, col[:, None] * inv_freq], axis=-1)
    return jnp.cos(ang), jnp.sin(ang)


def _with_ctx_identity(cos, sin, n_ctx):
    return (jnp.concatenate([cos, jnp.ones((n_ctx, cos.shape[1]), _F32)], axis=0),
            jnp.concatenate([sin, jnp.zeros((n_ctx, sin.shape[1]), _F32)], axis=0))


def _bf16_cols(w, width=None):
    w = w.astype(_BF16)
    if width is not None and width != w.shape[1]:
        w = jnp.pad(w, ((0, 0), (0, width - w.shape[1])))
    return w


def _round_up(n, mult):
    return (n + mult - 1) // mult * mult


def _gla_layer(h, w_in, w_a2, b_a, b_r, norm_g, w_out, n_lat, m_out):
    dk = w_a2.shape[-1]
    dv = b_r.shape[-1]
    rank = w_a2.shape[1]
    qk = mm(h, _bf16_cols(w_in[:, :2 * dk]), _F32, name="gla_qk")
    v = mm(h, _bf16_cols(w_in[:, 2 * dk:2 * dk + dv]), _BF16, name="gla_v")
    r = mm(h, _bf16_cols(w_in[:, 2 * dk + dv:2 * dk + 2 * dv]), _F32, m_rows=m_out, name="gla_r")
    a = mm(h, _bf16_cols(w_in[:, 2 * dk + 2 * dv:], LANES), _F32, name="gla_a")
    assert 2 * rank <= LANES
    o_f = gla_core(qk, v, a, w_a2, b_a[:, None, :], n_lat, reverse=False)
    o_b = gla_core(qk, v, a, w_a2, b_a[:, None, :], n_lat, reverse=True)
    gated = gla_gate(o_f, o_b, r, b_r[None], norm_g[None], m_out)
    return mm(gated, _bf16_cols(w_out), _F32, name="gla_out")


def _ret_layer(h, w_in, decay, gn_w, gn_b, w_out, rope, n_lat, m_out):
    heads = decay.shape[-1]
    dv = gn_w.shape[-1]
    dk = (w_in.shape[1] - 2 * dv) // 2
    qk = mm(h, _bf16_cols(w_in[:, :2 * dk]), _F32, name="ret_qk")
    v = mm(h, _bf16_cols(w_in[:, 2 * dk:2 * dk + dv]), _BF16, name="ret_v")
    g = mm(h, _bf16_cols(w_in[:, 2 * dk + dv:]), _F32, m_rows=m_out, name="ret_g")
    dec_b = jnp.broadcast_to(decay.astype(_F32)[:, :, None, None], (2, heads, 8, LANES))
    cos, sin = rope
    o_f = ret_core(qk, v, cos, sin, dec_b, n_lat, heads, reverse=False)
    o_b = ret_core(qk, v, cos, sin, dec_b, n_lat, heads, reverse=True)
    gated = ret_gate(o_f, o_b, g, gn_w[None], gn_b[None], heads, m_out)
    return mm(gated, _bf16_cols(w_out), _F32, name="ret_out")


def _mla_layer(h, w_down, q_norm, kv_norm, w_uq, w_ukv, q_gain, k_gain, w_out, rope2, n_lat, need_ctx):
    m = h.shape[0]
    n_ctx = m - n_lat
    q_rank, kv_rank = q_norm.shape[-1], kv_norm.shape[-1]
    qk_dim = q_gain.shape[-1]
    heads = w_uq.shape[1] // qk_dim
    dv = w_ukv.shape[1] // heads - MLA_NOPE
    assert qk_dim == MLA_NOPE + MLA_ROPE and heads % 2 == 0
    down_w = _round_up(w_down.shape[1], LANES)
    assert (q_rank + kv_rank) % LANES == 0
    down = mm(h, _bf16_cols(w_down, down_w), _F32, name="mla_down")
    cqn, ckvn = mla_norm(down, q_norm[None], kv_norm[None])
    w_uq3 = w_uq.reshape(q_rank, heads, qk_dim)
    w_uq_p = jnp.concatenate([w_uq3[:, :, :MLA_NOPE].reshape(q_rank, heads * MLA_NOPE),
                              w_uq3[:, :, MLA_NOPE:].reshape(q_rank, heads * MLA_ROPE)], axis=1)
    w_ukv3 = w_ukv.reshape(kv_rank, heads, MLA_NOPE + dv)
    w_uk = w_ukv3[:, :, :MLA_NOPE].reshape(kv_rank, heads * MLA_NOPE)
    w_uv = w_ukv3[:, :, MLA_NOPE:].reshape(kv_rank, heads * dv)
    qraw = mm(cqn, _bf16_cols(w_uq_p), _F32, name="mla_uq")
    knope = mm(ckvn, _bf16_cols(w_uk), _F32, name="mla_uk")
    v = mm(ckvn, _bf16_cols(w_uv), _BF16, name="mla_uv")
    cos2, sin2 = rope2
    q_hm, k_hm = mla_prep(qraw, knope, down, (q_rank + kv_rank) // LANES, cos2, sin2, q_gain, k_gain, heads)
    o_lat = attention(q_hm, k_hm, v, 0, n_lat, 0, m)
    if need_ctx:
        o_ctx = attention(q_hm, k_hm, v, n_lat, n_ctx, n_lat, n_ctx)
        o = jnp.concatenate([o_lat, o_ctx], axis=0)
    else:
        o = o_lat
    return mm(o, _bf16_cols(w_out), _F32, name="mla_out")


def kernel(x, c, ctx, c_ctx, ada_down, ada_up, gla_w_in, gla_w_a2, gla_b_a, gla_b_r, gla_norm, gla_w_out, ret_w_in, ret_decay, ret_gn_w, ret_gn_b, ret_w_out, mla_w_down, mla_q_norm, mla_kv_norm, mla_w_uq, mla_w_ukv, mla_q_gain, mla_k_gain, mla_w_out, ffn_w_in, ffn_conv_w, ffn_conv_b, ffn_w_out):
    assert x.shape[0] == 1 and ctx.shape[0] == 1 and c.shape[0] == 1
    n_lat, d = x.shape[1], x.shape[2]
    n_ctx = ctx.shape[1]
    depth = ada_down.shape[0]
    d_ff = ffn_conv_b.shape[-1]
    f_pad = _round_up(d_ff, 1024)

    xa = jnp.concatenate([x[0], ctx[0]], axis=0)
    cond8 = jnp.zeros((8, d), _F32).at[0].set(c[0]).at[1].set(c_ctx)
    mods = ada_all(cond8, ada_down, ada_up)[:, :2].reshape(depth, 2, 6, 1, d)

    ret_dkh = (ret_w_in.shape[-1] - 2 * ret_gn_w.shape[-1]) // 2 // ret_decay.shape[-1]
    ret_rope = _with_ctx_identity(*_axial_rope(n_lat, ret_dkh), n_ctx)
    cos_m, sin_m = _with_ctx_identity(*_axial_rope(n_lat, MLA_ROPE), n_ctx)
    mla_rope = (jnp.tile(cos_m, (1, 4)), jnp.tile(jnp.concatenate([-sin_m, sin_m], axis=1), (1, 2)))

    h = None
    for i in range(depth):
        kind, j = i % 3, i // 3
        need_ctx = i < depth - 1
        m_out = n_lat + n_ctx if need_ctx else n_lat
        sh1, sc1, g1, sh2, sc2, g2 = (mods[i, :, t] for t in range(6))
        if i == 0:
            (h,) = resmod(xa, n_lat, shift=sh1, scale=sc1)
        if kind == 0:
            y = _gla_layer(h, gla_w_in[j], gla_w_a2[j], gla_b_a[j], gla_b_r[j], gla_norm[j], gla_w_out[j],
                           n_lat, m_out)
        elif kind == 1:
            y = _ret_layer(h, ret_w_in[j], ret_decay[j], ret_gn_w[j], ret_gn_b[j], ret_w_out[j], ret_rope,
                           n_lat, m_out)
        else:
            y = _mla_layer(h, mla_w_down[j], mla_q_norm[j], mla_kv_norm[j], mla_w_uq[j], mla_w_ukv[j],
                           mla_q_gain[j], mla_k_gain[j], mla_w_out[j], mla_rope, n_lat, need_ctx)
        xa, h = resmod(xa, n_lat, y=y, gate=g1, shift=sh2, scale=sc2, m_rows=m_out)

        w_in = ffn_w_in[i]
        w_gu = jnp.concatenate([_bf16_cols(w_in[:, :d_ff], f_pad), _bf16_cols(w_in[:, d_ff:], f_pad)], axis=1)
        gu = mm(h, w_gu, _BF16, name="ffn_in")
        cw = jnp.pad(ffn_conv_w[i], ((0, 8 - ffn_conv_w.shape[1]), (0, f_pad - d_ff)))
        cb = jnp.pad(ffn_conv_b[i], (0, f_pad - d_ff))[None]
        act = convglu(gu, cw, cb, n_lat if need_ctx else m_out, f_pad)
        w_o = jnp.pad(ffn_w_out[i].astype(_BF16), ((0, f_pad - d_ff), (0, 0)))
        y = mm(act, w_o, _F32, name="ffn_out")
        if i + 1 < depth:
            sh1n, sc1n = mods[i + 1, :, 0], mods[i + 1, :, 1]
            xa, h = resmod(xa, n_lat, y=y, gate=g2, shift=sh1n, scale=sc1n, m_rows=m_out)
        else:
            (xa,) = resmod(xa, n_lat, y=y, gate=g2, m_rows=m_out)
    return xa[:n_lat][None]
```

```python
import functools
import math

import jax
import jax.numpy as jnp
from jax import lax
from jax.experimental import pallas as pl
from jax.experimental.pallas import tpu as pltpu

GRID_W = 64
EPS = 1e-6
ROPE_BASE = 10000.0
GLA_HEADS = 8
GLA_TAU = 16.0
MLA_NOPE = 128
MLA_ROPE = 64

LANES = 128
SUBLANES_BF16 = 16
VMEM_LIMIT_BYTES = 56 * 1024 * 1024

ROW_TILE = 256
GLA_CHUNK = 64
GLA_SUB = 16
SEQ_BLOCK = 256

_HIGHEST = lax.Precision.HIGHEST
_F32 = jnp.float32
_BF16 = jnp.bfloat16


def _pick_tile(n, cap, align):
    best = None
    d = align
    while d <= min(n, cap):
        if n % d == 0:
            best = d
        d += align
    return best if best is not None else n


def _params(sem, vmem=None):
    return pltpu.CompilerParams(dimension_semantics=sem, vmem_limit_bytes=vmem or VMEM_LIMIT_BYTES)


def _silu(x):
    return x * (1.0 / (1.0 + jnp.exp(-x)))


def _mm_fullk_kernel(a_ref, b_ref, o_ref):
    o_ref[...] = jnp.dot(a_ref[...], b_ref[...], preferred_element_type=_F32).astype(o_ref.dtype)


def _mm_ksplit_kernel(a_ref, b_ref, o_ref, acc_ref, *, nk):
    k = pl.program_id(2)

    @pl.when(k == 0)
    def _():
        acc_ref[...] = jnp.zeros_like(acc_ref)

    acc_ref[...] += jnp.dot(a_ref[...], b_ref[...], preferred_element_type=_F32)

    @pl.when(k == nk - 1)
    def _():
        o_ref[...] = acc_ref[...].astype(o_ref.dtype)


def mm(a, w, layer, out_dtype, m_rows=None, col0=0, n_cols=None, name="mm"):
    m = a.shape[0] if m_rows is None else m_rows
    _, k, n_all = w.shape
    n = n_all - col0 if n_cols is None else n_cols
    assert a.shape[1] == k and a.dtype == _BF16 and w.dtype == _BF16
    tm = _pick_tile(m, 1056, SUBLANES_BF16)
    tn = _pick_tile(n, 1024, LANES)
    if n > 1024 and tn < 512:
        tn = n
    assert col0 % tn == 0 and n % tn == 0
    j0 = col0 // tn
    tk = k if k <= 4096 else _pick_tile(k, 2816, LANES)
    out_shape = jax.ShapeDtypeStruct((m, n), out_dtype)
    if tk == k:
        return pl.pallas_call(
            _mm_fullk_kernel,
            grid=(m // tm, n // tn),
            in_specs=[pl.BlockSpec((tm, k), lambda i, j: (i, 0)),
                      pl.BlockSpec((None, k, tn), lambda i, j: (layer, 0, j0 + j))],
            out_specs=pl.BlockSpec((tm, tn), lambda i, j: (i, j)),
            out_shape=out_shape,
            compiler_params=_params(("parallel", "arbitrary")),
            name=name,
        )(a, w)
    nk = k // tk
    return pl.pallas_call(
        functools.partial(_mm_ksplit_kernel, nk=nk),
        grid=(m // tm, n // tn, nk),
        in_specs=[pl.BlockSpec((tm, tk), lambda i, j, kk: (i, kk)),
                  pl.BlockSpec((None, tk, tn), lambda i, j, kk: (layer, kk, j0 + j))],
        out_specs=pl.BlockSpec((tm, tn), lambda i, j, kk: (i, j)),
        out_shape=out_shape,
        scratch_shapes=[pltpu.VMEM((tm, tn), _F32)],
        compiler_params=_params(("parallel", "arbitrary", "arbitrary")),
        name=name,
    )(a, w)


def _ada_kernel(cond_ref, down_ref, up_ref, o_ref, t_ref):
    @pl.when(pl.program_id(1) == 0)
    def _():
        t_ref[...] = jnp.dot(_silu(cond_ref[...]), down_ref[...], precision=_HIGHEST,
                             preferred_element_type=_F32)

    o_ref[...] = jnp.dot(t_ref[...], up_ref[...], precision=_HIGHEST, preferred_element_type=_F32)


def ada_all(cond8, ada_down, ada_up):
    depth, d, r = ada_down.shape
    n = ada_up.shape[-1]
    tn = _pick_tile(n, 2048, LANES)
    return pl.pallas_call(
        _ada_kernel,
        grid=(depth, n // tn),
        in_specs=[pl.BlockSpec((8, d), lambda l, j: (0, 0)),
                  pl.BlockSpec((None, d, r), lambda l, j: (l, 0, 0)),
                  pl.BlockSpec((None, r, tn), lambda l, j: (l, 0, j))],
        out_specs=pl.BlockSpec((None, 8, tn), lambda l, j: (l, 0, j)),
        out_shape=jax.ShapeDtypeStruct((depth, 8, n), _F32),
        scratch_shapes=[pltpu.VMEM((8, r), _F32)],
        compiler_params=_params(("arbitrary", "arbitrary")),
        name="ada",
    )(cond8, ada_down, ada_up)


def _resmod_kernel(*refs, has_y, has_mod):
    it = iter(refs)
    x_ref = next(it)
    if has_y:
        y_ref, g_ref = next(it), next(it)
    if has_mod:
        sh_ref, sc_ref = next(it), next(it)
    if has_y:
        xo_ref = next(it)
    if has_mod:
        h_ref = next(it)
    x = x_ref[...]
    if has_y:
        x = x + g_ref[...] * y_ref[...]
        xo_ref[...] = x
    if has_mod:
        ms = jnp.mean(x * x, axis=-1, keepdims=True)
        h = x * lax.rsqrt(ms + EPS)
        h_ref[...] = (h * (1.0 + sc_ref[...]) + sh_ref[...]).astype(h_ref.dtype)


def resmod(xa, n_lat, y=None, gate=None, shift=None, scale=None, m_rows=None):
    m = xa.shape[0] if m_rows is None else m_rows
    d = xa.shape[1]
    tr = ROW_TILE
    assert m % tr == 0 and n_lat % tr == 0
    has_y, has_mod = y is not None, shift is not None
    n_lat_blocks = n_lat // tr
    row = pl.BlockSpec((tr, d), lambda i: (i, 0))
    vec = pl.BlockSpec((None, 1, d), lambda i: (jnp.where(i >= n_lat_blocks, 1, 0), 0, 0))
    args, in_specs, out_shape, out_specs = [xa], [row], [], []
    if has_y:
        args += [y, gate]
        in_specs += [row, vec]
        out_shape.append(jax.ShapeDtypeStruct((m, d), _F32))
        out_specs.append(row)
    if has_mod:
        args += [shift, scale]
        in_specs += [vec, vec]
        out_shape.append(jax.ShapeDtypeStruct((m, d), _BF16))
        out_specs.append(row)
    outs = pl.pallas_call(
        functools.partial(_resmod_kernel, has_y=has_y, has_mod=has_mod),
        grid=(m // tr,),
        in_specs=in_specs,
        out_specs=out_specs,
        out_shape=out_shape,
        compiler_params=_params(("parallel",)),
        name="resmod",
    )(*args)
    return outs


def _convglu_kernel(g_ref, u_ref, gp_ref, gn_ref, w_ref, b_ref, o_ref, *, tr, n_lat_blocks, n_blocks):
    i = pl.program_id(0)
    g = g_ref[...].astype(_F32)
    hs = SUBLANES_BF16
    starts_seq = jnp.logical_or(i == 0, i == n_lat_blocks)
    ends_seq = jnp.logical_or(i == n_lat_blocks - 1, i == n_blocks - 1)
    prev_row = jnp.where(starts_seq, 0.0, gp_ref[hs - 1:hs, :].astype(_F32))
    next_row = jnp.where(ends_seq, 0.0, gn_ref[0:1, :].astype(_F32))
    rows = lax.broadcasted_iota(jnp.int32, g.shape, 0)
    g_prev = jnp.where(rows == 0, prev_row, pltpu.roll(g, 1, axis=0))
    g_next = jnp.where(rows == tr - 1, next_row, pltpu.roll(g, tr - 1, axis=0))
    w = w_ref[...]
    z = g_prev * w[0:1, :] + g * w[1:2, :] + g_next * w[2:3, :] + b_ref[...]
    c0 = math.sqrt(2.0 / math.pi)
    gelu = 0.5 * z * (1.0 + jnp.tanh(c0 * (z + 0.044715 * (z * z * z))))
    o_ref[...] = (gelu * u_ref[...].astype(_F32)).astype(o_ref.dtype)


def convglu(gu, conv_w, conv_b, n_lat, f_pad):
    m = gu.shape[0]
    tr = ROW_TILE
    tc = _pick_tile(f_pad, 1408, LANES)
    hs = SUBLANES_BF16
    ncb = f_pad // tc
    n_blocks = m // tr
    last_halo = m // hs - 1
    return pl.pallas_call(
        functools.partial(_convglu_kernel, tr=tr, n_lat_blocks=n_lat // tr, n_blocks=n_blocks),
        grid=(n_blocks, ncb),
        in_specs=[pl.BlockSpec((tr, tc), lambda i, j: (i, j)),
                  pl.BlockSpec((tr, tc), lambda i, j: (i, j + ncb)),
                  pl.BlockSpec((hs, tc), lambda i, j: (jnp.maximum(i * (tr // hs) - 1, 0), j)),
                  pl.BlockSpec((hs, tc), lambda i, j: (jnp.minimum((i + 1) * (tr // hs), last_halo), j)),
                  pl.BlockSpec((8, tc), lambda i, j: (0, j)),
                  pl.BlockSpec((1, tc), lambda i, j: (0, j))],
        out_specs=pl.BlockSpec((tr, tc), lambda i, j: (i, j)),
        out_shape=jax.ShapeDtypeStruct((m, f_pad), _BF16),
        compiler_params=_params(("parallel", "arbitrary")),
        name="convglu",
    )(gu, gu, gu, gu, conv_w, conv_b)


def _seq_block_index(c, n_lat_blocks, n_ctx_blocks, reverse):
    if reverse:
        ctx = n_lat_blocks + n_ctx_blocks - 1 - c
        lat = n_lat_blocks - 1 - (c - n_ctx_blocks)
    else:
        ctx = n_lat_blocks + c
        lat = c - n_ctx_blocks
    return jnp.where(c < n_ctx_blocks, ctx, lat)


def _gla_kernel(q_ref, k_ref, v_ref, a_ref, w2_ref, ba_ref, o_ref, s_ref, *, reverse, rank, scale):
    C, SUB, TB = GLA_CHUNK, GLA_SUB, SEQ_BLOCK
    HALF = SUB // 2
    n_chunks = TB // C
    n_sub = C // SUB
    dv = v_ref.shape[1]

    @pl.when(pl.program_id(1) == 0)
    def _():
        s_ref[...] = jnp.zeros_like(s_ref)

    lane0 = rank if reverse else 0
    a = a_ref[:, lane0:lane0 + rank]
    z = jnp.dot(a, w2_ref[...], precision=_HIGHEST, preferred_element_type=_F32) + ba_ref[...]
    la = (jnp.minimum(z, 0.0) - jnp.log1p(jnp.exp(-jnp.abs(z)))) * (1.0 / GLA_TAU)
    t_idx = lax.broadcasted_iota(jnp.int32, (TB, TB), 0)
    u_idx = lax.broadcasted_iota(jnp.int32, (TB, TB), 1)
    same_chunk = (t_idx // C) == (u_idx // C)
    along = (u_idx >= t_idx) if reverse else (u_idx <= t_idx)
    cumsum_op = jnp.where(jnp.logical_and(same_chunk, along), 1.0, 0.0).astype(_F32)
    b_all = jnp.dot(cumsum_op, la, precision=_HIGHEST, preferred_element_type=_F32)
    sel_t = lax.broadcasted_iota(jnp.int32, (TB, n_chunks * LANES), 0) // C
    sel_c = lax.broadcasted_iota(jnp.int32, (TB, n_chunks * LANES), 1) // LANES
    chunk_sel = jnp.where(sel_t == sel_c, 1.0, 0.0).astype(_F32)
    tot_all = lax.dot_general(la, chunk_sel, (((0,), (0,)), ((), ())), precision=_HIGHEST,
                              preferred_element_type=_F32)
    q_all = q_ref[...] * scale
    k_all = k_ref[...]
    key_idx = lax.broadcasted_iota(jnp.int32, (HALF, C), 1)
    half_rows = lax.broadcasted_iota(jnp.int32, (HALF, 1), 0)

    s = s_ref[...]
    for ci in range(n_chunks):
        cidx = (n_chunks - 1 - ci) if reverse else ci
        rows = slice(cidx * C, (cidx + 1) * C)
        b, q, k, v = b_all[rows], q_all[rows], k_all[rows], v_ref[rows, :]
        o = jnp.dot((q * jnp.exp(b)).astype(_BF16), s.astype(_BF16), preferred_element_type=_F32)

        a_rows = []
        for i in range(n_sub):
            lo = i * SUB
            bi, qi, ki = b[lo:lo + SUB], q[lo:lo + SUB], k[lo:lo + SUB]
            has_earlier = (i < n_sub - 1) if reverse else (i > 0)
            if has_earlier:
                ref_row = lo + SUB - 1 if reverse else lo
                bref = b[ref_row:ref_row + 1]
                qe = qi * jnp.exp(bi - bref)
                ke = k * jnp.exp(jnp.minimum(bref - b, 0.0))
                off = lax.dot_general(qe.astype(_BF16), ke.astype(_BF16), (((1,), (1,)), ((), ())),
                                      preferred_element_type=_F32)
                earlier = (key_idx >= lo + SUB) if reverse else (key_idx < lo)
            for hf in range(2):
                h0 = hf * HALF
                bt, qt = bi[h0:h0 + HALF], qi[h0:h0 + HALF]
                ah = jnp.where(earlier, off[h0:h0 + HALF], 0.0) if has_earlier else jnp.zeros((HALF, C), _F32)
                s_range = range(h0, SUB) if reverse else range(0, h0 + HALF)
                for s_i in s_range:
                    dlt = bt - bi[s_i:s_i + 1]
                    if s_i // HALF == hf:
                        seen = (half_rows <= s_i - h0) if reverse else (half_rows >= s_i - h0)
                        dlt = jnp.where(seen, dlt, -jnp.inf)
                    wgt = qt * ki[s_i:s_i + 1] * jnp.exp(dlt)
                    col = jnp.sum(wgt, axis=1, keepdims=True)
                    ah = jnp.where(key_idx == lo + s_i, col, ah)
                a_rows.append(ah)
        amat = jnp.concatenate(a_rows, axis=0)
        o_ref[rows, :] = o + jnp.dot(amat.astype(_BF16), v, preferred_element_type=_F32)

        end_row = 0 if reverse else C - 1
        kd = (k * jnp.exp(b[end_row:end_row + 1] - b)).astype(_BF16)
        dec = jnp.exp(tot_all[:, cidx * LANES:(cidx + 1) * LANES])
        dec = jnp.concatenate([dec] * (dv // LANES), axis=1)
        s = dec * s + lax.dot_general(kd, v, (((0,), (0,)), ((), ())), preferred_element_type=_F32)
    s_ref[...] = s


def gla_core(qk, v, a, w_a2, b_a, n_lat, reverse):
    m = qk.shape[0]
    dk_all = qk.shape[1] // 2
    dv_all = v.shape[1]
    heads = GLA_HEADS
    dkh, dvh = dk_all // heads, dv_all // heads
    rank = w_a2.shape[1]
    tb = SEQ_BLOCK
    nlb, ncb = n_lat // tb, (m - n_lat) // tb
    d = 1 if reverse else 0
    blk = functools.partial(_seq_block_index, n_lat_blocks=nlb, n_ctx_blocks=ncb, reverse=reverse)
    return pl.pallas_call(
        functools.partial(_gla_kernel, reverse=reverse, rank=rank, scale=dkh ** -0.5),
        grid=(heads, nlb + ncb),
        in_specs=[pl.BlockSpec((tb, dkh), lambda h, c: (blk(c), h)),
                  pl.BlockSpec((tb, dkh), lambda h, c: (blk(c), heads + h)),
                  pl.BlockSpec((tb, dvh), lambda h, c: (blk(c), h)),
                  pl.BlockSpec((tb, LANES), lambda h, c: (blk(c), 0)),
                  pl.BlockSpec((None, rank, dkh), lambda h, c: (d, 0, h)),
                  pl.BlockSpec((None, 1, dkh), lambda h, c: (d, 0, h))],
        out_specs=pl.BlockSpec((tb, dvh), lambda h, c: (blk(c), h)),
        out_shape=jax.ShapeDtypeStruct((m, dv_all), _F32),
        scratch_shapes=[pltpu.VMEM((dkh, dvh), _F32)],
        compiler_params=_params(("parallel", "arbitrary")),
        name="gla_rev" if reverse else "gla_fwd",
    )(qk, qk, v, a, w_a2, b_a)


def _gla_gate_kernel(of_ref, ob_ref, r_ref, br_ref, g_ref, o_ref):
    o = of_ref[...] + ob_ref[...]
    o = o * lax.rsqrt(jnp.mean(o * o, axis=-1, keepdims=True) + EPS) * g_ref[...]
    o_ref[...] = (_silu(r_ref[...] + br_ref[...]) * o).astype(o_ref.dtype)


def gla_gate(o_f, o_b, r, b_r, norm_g, m_rows):
    dv_all = o_f.shape[1]
    dvh = norm_g.shape[-1]
    tr = ROW_TILE
    blk = pl.BlockSpec((tr, dvh), lambda i, h: (i, h))
    return pl.pallas_call(
        _gla_gate_kernel,
        grid=(m_rows // tr, dv_all // dvh),
        in_specs=[blk, blk, blk,
                  pl.BlockSpec((1, dvh), lambda i, h: (0, h)),
                  pl.BlockSpec((1, dvh), lambda i, h: (0, 0))],
        out_specs=blk,
        out_shape=jax.ShapeDtypeStruct((m_rows, dv_all), _BF16),
        compiler_params=_params(("parallel", "arbitrary")),
        name="gla_gate",
    )(o_f, o_b, r, b_r, norm_g)


def _ret_kernel(q_ref, k_ref, v_ref, cos_ref, sin_ref, e_ref, o_ref, s_ref, *, reverse, scale):
    C = SEQ_BLOCK
    dk = q_ref.shape[1]
    dv = v_ref.shape[1]
    half = dk // 2

    @pl.when(pl.program_id(1) == 0)
    def _():
        s_ref[...] = jnp.zeros_like(s_ref)

    lg = jnp.log1p(-jnp.exp2(-e_ref[...]))[0:1, 0:1]
    cos, sin = cos_ref[...], sin_ref[...]

    def rope(x):
        x1, x2 = x[:, :half], x[:, half:]
        return jnp.concatenate([x1 * cos - x2 * sin, x1 * sin + x2 * cos], axis=1)

    q = rope(q_ref[...] * scale)
    k = rope(k_ref[...])
    v = v_ref[...]
    t_idx = lax.broadcasted_iota(jnp.int32, (C, C), 0).astype(_F32)
    s_idx = lax.broadcasted_iota(jnp.int32, (C, C), 1).astype(_F32)
    rel = (s_idx - t_idx) if reverse else (t_idx - s_idx)
    decay = jnp.exp(jnp.where(rel >= 0, lg * rel, -jnp.inf))
    pos = lax.broadcasted_iota(jnp.int32, (C, dk), 0).astype(_F32)
    if reverse:
        q_dec, k_dec = jnp.exp(lg * (C - pos)), jnp.exp(lg * pos)
    else:
        q_dec, k_dec = jnp.exp(lg * (pos + 1.0)), jnp.exp(lg * (C - 1.0 - pos))
    s = s_ref[...]
    scores = lax.dot_general(q.astype(_BF16), k.astype(_BF16), (((1,), (1,)), ((), ())),
                             preferred_element_type=_F32) * decay
    o = jnp.dot(scores.astype(_BF16), v, preferred_element_type=_F32)
    o = o + jnp.dot((q * q_dec).astype(_BF16), s.astype(_BF16), preferred_element_type=_F32)
    o_ref[...] = o
    state_dec = jnp.exp(lg * float(C))
    s_ref[...] = state_dec * s + lax.dot_general((k * k_dec).astype(_BF16), v, (((0,), (0,)), ((), ())),
                                                 preferred_element_type=_F32)


def ret_core(qk, v, cos, sin, dec_b, n_lat, heads, reverse):
    m = qk.shape[0]
    dkh = qk.shape[1] // 2 // heads
    dvh = v.shape[1] // heads
    tb = SEQ_BLOCK
    nlb, ncb = n_lat // tb, (m - n_lat) // tb
    d = 1 if reverse else 0
    blk = functools.partial(_seq_block_index, n_lat_blocks=nlb, n_ctx_blocks=ncb, reverse=reverse)
    return pl.pallas_call(
        functools.partial(_ret_kernel, reverse=reverse, scale=dkh ** -0.5),
        grid=(heads, nlb + ncb),
        in_specs=[pl.BlockSpec((tb, dkh), lambda h, c: (blk(c), h)),
                  pl.BlockSpec((tb, dkh), lambda h, c: (blk(c), heads + h)),
                  pl.BlockSpec((tb, dvh), lambda h, c: (blk(c), h)),
                  pl.BlockSpec((tb, dkh // 2), lambda h, c: (blk(c), 0)),
                  pl.BlockSpec((tb, dkh // 2), lambda h, c: (blk(c), 0)),
                  pl.BlockSpec((None, None, 8, LANES), lambda h, c: (d, h, 0, 0))],
        out_specs=pl.BlockSpec((tb, dvh), lambda h, c: (blk(c), h)),
        out_shape=jax.ShapeDtypeStruct((m, v.shape[1]), _F32),
        scratch_shapes=[pltpu.VMEM((dkh, dvh), _F32)],
        compiler_params=_params(("parallel", "arbitrary")),
        name="ret_rev" if reverse else "ret_fwd",
    )(qk, qk, v, cos, sin, dec_b)


def _ret_gate_kernel(of_ref, ob_ref, g_ref, w_ref, b_ref, o_ref):
    o = of_ref[...] + ob_ref[...]
    mu = jnp.mean(o, axis=-1, keepdims=True)
    oc = o - mu
    var = jnp.mean(oc * oc, axis=-1, keepdims=True)
    o = oc * lax.rsqrt(var + EPS) * w_ref[...] + b_ref[...]
    o_ref[...] = (_silu(g_ref[...]) * o).astype(o_ref.dtype)


def ret_gate(o_f, o_b, g, gn_w, gn_b, heads, m_rows):
    dv_all = o_f.shape[1]
    dvh = dv_all // heads
    tr = ROW_TILE
    blk = pl.BlockSpec((tr, dvh), lambda i, h: (i, h))
    vec = pl.BlockSpec((1, dvh), lambda i, h: (0, h))
    return pl.pallas_call(
        _ret_gate_kernel,
        grid=(m_rows // tr, heads),
        in_specs=[blk, blk, blk, vec, vec],
        out_specs=blk,
        out_shape=jax.ShapeDtypeStruct((m_rows, dv_all), _BF16),
        compiler_params=_params(("parallel", "arbitrary")),
        name="ret_gate",
    )(o_f, o_b, g, gn_w, gn_b)


def _mla_norm_kernel(cq_ref, ckv_ref, qn_ref, kvn_ref, oq_ref, okv_ref):
    cq = cq_ref[...]
    oq_ref[...] = (cq * lax.rsqrt(jnp.mean(cq * cq, axis=-1, keepdims=True) + EPS) * qn_ref[...]).astype(oq_ref.dtype)
    ckv = ckv_ref[...]
    okv_ref[...] = (ckv * lax.rsqrt(jnp.mean(ckv * ckv, axis=-1, keepdims=True) + EPS)
                    * kvn_ref[...]).astype(okv_ref.dtype)


def mla_norm(down, q_norm, kv_norm):
    m = down.shape[0]
    qr, kvr = q_norm.shape[-1], kv_norm.shape[-1]
    assert qr % kvr == 0
    tr = ROW_TILE
    return pl.pallas_call(
        _mla_norm_kernel,
        grid=(m // tr,),
        in_specs=[pl.BlockSpec((tr, qr), lambda i: (i, 0)),
                  pl.BlockSpec((tr, kvr), lambda i: (i, qr // kvr)),
                  pl.BlockSpec((1, qr), lambda i: (0, 0)),
                  pl.BlockSpec((1, kvr), lambda i: (0, 0))],
        out_specs=[pl.BlockSpec((tr, qr), lambda i: (i, 0)), pl.BlockSpec((tr, kvr), lambda i: (i, 0))],
        out_shape=[jax.ShapeDtypeStruct((m, qr), _BF16), jax.ShapeDtypeStruct((m, kvr), _BF16)],
        compiler_params=_params(("parallel",)),
        name="mla_norm",
    )(down, down, q_norm, kv_norm)


def _swap_rope_halves(x):
    lane = lax.broadcasted_iota(jnp.int32, x.shape, 1)
    return jnp.where((lane & 32) == 0, pltpu.roll(x, LANES - 32, axis=1), pltpu.roll(x, 32, axis=1))


def _mla_prep_kernel(qn_ref, qr_ref, kn_ref, kr_ref, v_ref, c_ref, s_ref, gqn_ref, gqr_ref, gkn_ref, gkr_ref,
                     oq_ref, ok_ref, ov_ref, *, qk_dim, scale, dv):
    cosv, sinv = c_ref[...], s_ref[...]
    lane = lax.broadcasted_iota(jnp.int32, cosv.shape, 1)
    low = lane < MLA_ROPE
    qr = qr_ref[...]
    kr = kr_ref[...]
    kr_sq = jnp.sum(kr * kr, axis=-1, keepdims=True)
    kr_g = kr * gkr_ref[...]
    kr_rot = kr_g * cosv + _swap_rope_halves(kr_g) * sinv
    for hh in range(2):
        qn = qn_ref[:, hh * MLA_NOPE:(hh + 1) * MLA_NOPE]
        mine = low if hh == 0 else jnp.logical_not(low)
        q_sq = jnp.sum(qn * qn, axis=-1, keepdims=True) + jnp.sum(jnp.where(mine, qr * qr, 0.0), axis=-1, keepdims=True)
        q_rr = lax.rsqrt(q_sq * (1.0 / qk_dim) + EPS) * scale
        qr_g = qr * q_rr * gqr_ref[...]
        qr_rot = qr_g * cosv + _swap_rope_halves(qr_g) * sinv
        if hh == 1:
            qr_rot = pltpu.roll(qr_rot, MLA_ROPE, axis=1)
        q_full = jnp.concatenate([qn * q_rr * gqn_ref[...], jnp.where(low, qr_rot, 0.0)], axis=1)
        oq_ref[hh] = q_full.T.astype(oq_ref.dtype)

        kn = kn_ref[:, hh * MLA_NOPE:(hh + 1) * MLA_NOPE]
        k_sq = jnp.sum(kn * kn, axis=-1, keepdims=True) + kr_sq
        k_rr = lax.rsqrt(k_sq * (1.0 / qk_dim) + EPS)
        ok_ref[hh, :, 0:MLA_NOPE] = (kn * k_rr * gkn_ref[...]).astype(ok_ref.dtype)
        ok_ref[hh, :, MLA_NOPE:2 * MLA_NOPE] = jnp.where(low, kr_rot * k_rr, 0.0).astype(ok_ref.dtype)

        ov_ref[hh, 0:dv, :] = v_ref[:, hh * dv:(hh + 1) * dv].astype(_F32).T.astype(ov_ref.dtype)
        ov_ref[hh, dv:, :] = jnp.ones((ov_ref.shape[1] - dv, ov_ref.shape[2]), ov_ref.dtype)


def mla_prep(qraw, knope, down, kr_block, v, cos2, sin2, q_gain, k_gain, heads):
    m = qraw.shape[0]
    qk_dim = MLA_NOPE + MLA_ROPE
    dv = v.shape[1] // heads
    assert dv == MLA_NOPE
    tr = _pick_tile(m, 1024, LANES)
    nope_blocks = heads * MLA_NOPE // (2 * MLA_NOPE)
    gqn, gkn = q_gain[None, :MLA_NOPE], k_gain[None, :MLA_NOPE]
    gqr = jnp.tile(q_gain[MLA_NOPE:], 2)[None]
    gkr = jnp.tile(k_gain[MLA_NOPE:], 2)[None]
    vec = pl.BlockSpec((1, LANES), lambda i, p: (0, 0))
    dve = dv + SUBLANES_BF16
    return pl.pallas_call(
        functools.partial(_mla_prep_kernel, qk_dim=qk_dim, scale=qk_dim ** -0.5 * math.log2(math.e), dv=dv),
        grid=(m // tr, heads // 2),
        in_specs=[pl.BlockSpec((tr, 2 * MLA_NOPE), lambda i, p: (i, p)),
                  pl.BlockSpec((tr, LANES), lambda i, p: (i, 2 * nope_blocks + p)),
                  pl.BlockSpec((tr, 2 * MLA_NOPE), lambda i, p: (i, p)),
                  pl.BlockSpec((tr, LANES), lambda i, p: (i, kr_block)),
                  pl.BlockSpec((tr, 2 * dv), lambda i, p: (i, p)),
                  pl.BlockSpec((tr, LANES), lambda i, p: (i, 0)),
                  pl.BlockSpec((tr, LANES), lambda i, p: (i, 0)),
                  vec, vec, vec, vec],
        out_specs=[pl.BlockSpec((2, 2 * MLA_NOPE, tr), lambda i, p: (p, 0, i)),
                   pl.BlockSpec((2, tr, 2 * MLA_NOPE), lambda i, p: (p, i, 0)),
                   pl.BlockSpec((2, dve, tr), lambda i, p: (p, 0, i))],
        out_shape=[jax.ShapeDtypeStruct((heads, 2 * MLA_NOPE, m), _BF16),
                   jax.ShapeDtypeStruct((heads, m, 2 * MLA_NOPE), _BF16),
                   jax.ShapeDtypeStruct((heads, dve, m), _BF16)],
        compiler_params=_params(("parallel", "arbitrary")),
        name="mla_prep",
    )(qraw, qraw, knope, down, v, cos2, sin2, gqn, gqr, gkn, gkr)


def _attn_kernel(qt_ref, k_ref, vt_ref, o_ref, acc_ref, *, tkv, n_kv, dv):
    tq = qt_ref.shape[1]
    qt = qt_ref[...]
    acc_ref[...] = jnp.zeros_like(acc_ref)
    m_run = jnp.full((1, tq), -jnp.inf, _F32)
    for j in range(n_kv):
        rows = slice(j * tkv, (j + 1) * tkv)
        s = jnp.dot(k_ref[rows, :], qt, preferred_element_type=_F32)
        m_new = jnp.maximum(m_run, jnp.max(s, axis=0, keepdims=True))
        alpha = jnp.exp2(m_run - m_new)
        p = jnp.exp2(s - m_new).astype(_BF16)
        acc_ref[...] = alpha * acc_ref[...] + jnp.dot(vt_ref[:, rows], p, preferred_element_type=_F32)
        m_run = m_new
    acc = acc_ref[...]
    o_ref[...] = (acc[:dv] / acc[dv:dv + 1]).T.astype(o_ref.dtype)


def attention(qt_hm, k_hm, vt_hm, q_row0, n_q, kv_row0, n_kv_rows):
    heads, dqk, _ = qt_hm.shape
    dve = vt_hm.shape[1]
    dv = dve - SUBLANES_BF16
    tq = _pick_tile(n_q, 2048, LANES)
    tkv = _pick_tile(n_kv_rows, 384, LANES)
    assert q_row0 % tq == 0 and kv_row0 % n_kv_rows == 0
    q0, kv0 = q_row0 // tq, kv_row0 // n_kv_rows
    return pl.pallas_call(
        functools.partial(_attn_kernel, tkv=tkv, n_kv=n_kv_rows // tkv, dv=dv),
        grid=(heads, n_q // tq),
        in_specs=[pl.BlockSpec((None, dqk, tq), lambda h, i: (h, 0, q0 + i)),
                  pl.BlockSpec((None, n_kv_rows, dqk), lambda h, i: (h, kv0, 0)),
                  pl.BlockSpec((None, dve, n_kv_rows), lambda h, i: (h, 0, kv0))],
        out_specs=pl.BlockSpec((tq, dv), lambda h, i: (i, h)),
        out_shape=jax.ShapeDtypeStruct((n_q, heads * dv), _BF16),
        scratch_shapes=[pltpu.VMEM((dve, tq), _F32)],
        compiler_params=_params(("parallel", "arbitrary")),
        name="attn",
    )(qt_hm, k_hm, vt_hm)


def _axial_rope(n_tokens, dim):
    rows = n_tokens // GRID_W
    quarter = dim // 4
    inv_freq = ROPE_BASE ** (-jnp.arange(quarter, dtype=_F32) / quarter)
    row = jnp.repeat(jnp.arange(rows, dtype=_F32), GRID_W)
    col = jnp.tile(jnp.arange(GRID_W, dtype=_F32), rows)
    ang = jnp.concatenate([row[:, None] * inv_freq, col[:, None] * inv_freq], axis=-1)
    return jnp.cos(ang), jnp.sin(ang)


def _with_ctx_identity(cos, sin, n_ctx):
    return (jnp.concatenate([cos, jnp.ones((n_ctx, cos.shape[1]), _F32)], axis=0),
            jnp.concatenate([sin, jnp.zeros((n_ctx, sin.shape[1]), _F32)], axis=0))


def _pad_last(w, width):
    return w if width == w.shape[-1] else jnp.pad(w, [(0, 0)] * (w.ndim - 1) + [(0, width - w.shape[-1])])


def _round_up(n, mult):
    return (n + mult - 1) // mult * mult


def _gla_layer(h, j, w_in, w_a, w_a2, b_a, b_r, norm_g, w_out, n_lat, m_out):
    dk = w_a2.shape[-1]
    dv = b_r.shape[-1]
    rank = w_a2.shape[1]
    assert 2 * rank <= LANES
    qk = mm(h, w_in, j, _F32, col0=0, n_cols=2 * dk, name="gla_qk")
    v = mm(h, w_in, j, _BF16, col0=2 * dk, n_cols=dv, name="gla_v")
    r = mm(h, w_in, j, _F32, m_rows=m_out, col0=2 * dk + dv, n_cols=dv, name="gla_r")
    a = mm(h, w_a, j, _F32, name="gla_a")
    o_f = gla_core(qk, v, a, w_a2, b_a[:, None, :], n_lat, reverse=False)
    o_b = gla_core(qk, v, a, w_a2, b_a[:, None, :], n_lat, reverse=True)
    gated = gla_gate(o_f, o_b, r, b_r[None], norm_g[None], m_out)
    return mm(gated, w_out, j, _F32, name="gla_out")


def _ret_layer(h, j, w_in, decay, gn_w, gn_b, w_out, rope, n_lat, m_out):
    heads = decay.shape[-1]
    dv = gn_w.shape[-1]
    dk = (w_in.shape[-1] - 2 * dv) // 2
    qk = mm(h, w_in, j, _F32, col0=0, n_cols=2 * dk, name="ret_qk")
    v = mm(h, w_in, j, _BF16, col0=2 * dk, n_cols=dv, name="ret_v")
    g = mm(h, w_in, j, _F32, m_rows=m_out, col0=2 * dk + dv, n_cols=dv, name="ret_g")
    dec_b = jnp.broadcast_to(decay.astype(_F32)[:, :, None, None], (2, heads, 8, LANES))
    cos, sin = rope
    o_f = ret_core(qk, v, cos, sin, dec_b, n_lat, heads, reverse=False)
    o_b = ret_core(qk, v, cos, sin, dec_b, n_lat, heads, reverse=True)
    gated = ret_gate(o_f, o_b, g, gn_w[None], gn_b[None], heads, m_out)
    return mm(gated, w_out, j, _F32, name="ret_out")


def _mla_weights(w_down, w_uq, w_ukv, heads, qk_dim):
    n, q_rank, kv_rank = w_uq.shape[0], w_uq.shape[1], w_ukv.shape[1]
    w_uq4 = w_uq.astype(_BF16).reshape(n, q_rank, heads, qk_dim)
    w_uq_p = jnp.concatenate([w_uq4[..., :MLA_NOPE].reshape(n, q_rank, heads * MLA_NOPE),
                              w_uq4[..., MLA_NOPE:].reshape(n, q_rank, heads * MLA_ROPE)], axis=-1)
    w_ukv4 = w_ukv.astype(_BF16).reshape(n, kv_rank, heads, -1)
    w_uk = w_ukv4[..., :MLA_NOPE].reshape(n, kv_rank, heads * MLA_NOPE)
    w_uv = w_ukv4[..., MLA_NOPE:].reshape(n, kv_rank, -1)
    w_dn = _pad_last(w_down.astype(_BF16), _round_up(w_down.shape[-1], LANES))
    return w_dn, w_uq_p, w_uk, w_uv


def _mla_layer(h, j, w_dn, q_norm, kv_norm, w_uq_p, w_uk, w_uv, q_gain, k_gain, w_out, rope2, n_lat, need_ctx):
    m = h.shape[0]
    n_ctx = m - n_lat
    q_rank, kv_rank = q_norm.shape[-1], kv_norm.shape[-1]
    qk_dim = q_gain.shape[-1]
    heads = w_uk.shape[-1] // MLA_NOPE
    assert qk_dim == MLA_NOPE + MLA_ROPE and heads % 2 == 0 and (q_rank + kv_rank) % LANES == 0
    down = mm(h, w_dn, j, _F32, name="mla_down")
    cqn, ckvn = mla_norm(down, q_norm[None], kv_norm[None])
    qraw = mm(cqn, w_uq_p, j, _F32, name="mla_uq")
    knope = mm(ckvn, w_uk, j, _F32, name="mla_uk")
    v = mm(ckvn, w_uv, j, _BF16, name="mla_uv")
    cos2, sin2 = rope2
    qt_hm, k_hm, vt_hm = mla_prep(qraw, knope, down, (q_rank + kv_rank) // LANES, v, cos2, sin2, q_gain, k_gain,
                                  heads)
    o_lat = attention(qt_hm, k_hm, vt_hm, 0, n_lat, 0, m)
    if need_ctx:
        o_ctx = attention(qt_hm, k_hm, vt_hm, n_lat, n_ctx, n_lat, n_ctx)
        o = jnp.concatenate([o_lat, o_ctx], axis=0)
    else:
        o = o_lat
    return mm(o, w_out, j, _F32, name="mla_out")


def kernel(x, c, ctx, c_ctx, ada_down, ada_up, gla_w_in, gla_w_a2, gla_b_a, gla_b_r, gla_norm, gla_w_out, ret_w_in, ret_decay, ret_gn_w, ret_gn_b, ret_w_out, mla_w_down, mla_q_norm, mla_kv_norm, mla_w_uq, mla_w_ukv, mla_q_gain, mla_k_gain, mla_w_out, ffn_w_in, ffn_conv_w, ffn_conv_b, ffn_w_out):
    assert x.shape[0] == 1 and ctx.shape[0] == 1 and c.shape[0] == 1
    n_lat, d = x.shape[1], x.shape[2]
    n_ctx = ctx.shape[1]
    depth = ada_down.shape[0]
    d_ff = ffn_conv_b.shape[-1]
    f_pad = _round_up(d_ff, 1024)

    xa = jnp.concatenate([x[0], ctx[0]], axis=0)
    cond8 = jnp.zeros((8, d), _F32).at[0].set(c[0]).at[1].set(c_ctx)
    mods = ada_all(cond8, ada_down, ada_up)[:, :2].reshape(depth, 2, 6, 1, d)

    ret_dkh = (ret_w_in.shape[-1] - 2 * ret_gn_w.shape[-1]) // 2 // ret_decay.shape[-1]
    ret_rope = _with_ctx_identity(*_axial_rope(n_lat, ret_dkh), n_ctx)
    cos_m, sin_m = _with_ctx_identity(*_axial_rope(n_lat, MLA_ROPE), n_ctx)
    mla_rope = (jnp.tile(cos_m, (1, 4)), jnp.tile(jnp.concatenate([-sin_m, sin_m], axis=1), (1, 2)))

    gla_in_b = gla_w_in.astype(_BF16)
    gla_rank = gla_w_a2.shape[2]
    gla_a_b = _pad_last(gla_w_in[:, :, gla_w_in.shape[-1] - 2 * gla_rank:].astype(_BF16), LANES)
    gla_out_b = gla_w_out.astype(_BF16)
    ret_in_b, ret_out_b = ret_w_in.astype(_BF16), ret_w_out.astype(_BF16)
    if mla_w_uq.shape[0]:
        mla_heads = mla_w_uq.shape[-1] // mla_q_gain.shape[-1]
        mla_dn_b, mla_uq_b, mla_uk_b, mla_uv_b = _mla_weights(mla_w_down, mla_w_uq, mla_w_ukv, mla_heads,
                                                              mla_q_gain.shape[-1])
        mla_out_b = mla_w_out.astype(_BF16)
    ffn_in_b = _pad_last(ffn_w_in.astype(_BF16).reshape(depth, d, 2, d_ff), f_pad).reshape(depth, d, 2 * f_pad)
    ffn_out_b = jnp.pad(ffn_w_out.astype(_BF16), ((0, 0), (0, f_pad - d_ff), (0, 0)))
    conv_w8 = jnp.pad(ffn_conv_w, ((0, 0), (0, 8 - ffn_conv_w.shape[1]), (0, f_pad - d_ff)))
    conv_b1 = _pad_last(ffn_conv_b, f_pad)[:, None, :]

    h = None
    for i in range(depth):
        kind, j = i % 3, i // 3
        need_ctx = i < depth - 1
        m_out = n_lat + n_ctx if need_ctx else n_lat
        sh1, sc1, g1, sh2, sc2, g2 = (mods[i, :, t] for t in range(6))
        if i == 0:
            (h,) = resmod(xa, n_lat, shift=sh1, scale=sc1)
        if kind == 0:
            y = _gla_layer(h, j, gla_in_b, gla_a_b, gla_w_a2[j], gla_b_a[j], gla_b_r[j], gla_norm[j], gla_out_b,
                           n_lat, m_out)
        elif kind == 1:
            y = _ret_layer(h, j, ret_in_b, ret_decay[j], ret_gn_w[j], ret_gn_b[j], ret_out_b, ret_rope,
                           n_lat, m_out)
        else:
            y = _mla_layer(h, j, mla_dn_b, mla_q_norm[j], mla_kv_norm[j], mla_uq_b, mla_uk_b, mla_uv_b,
                           mla_q_gain[j], mla_k_gain[j], mla_out_b, mla_rope, n_lat, need_ctx)
        xa, h = resmod(xa, n_lat, y=y, gate=g1, shift=sh2, scale=sc2, m_rows=m_out)

        gu = mm(h, ffn_in_b, i, _BF16, name="ffn_in")
        act = convglu(gu, conv_w8[i], conv_b1[i], n_lat if need_ctx else m_out, f_pad)
        y = mm(act, ffn_out_b, i, _F32, name="ffn_out")
        if i + 1 < depth:
            sh1n, sc1n = mods[i + 1, :, 0], mods[i + 1, :, 1]
            xa, h = resmod(xa, n_lat, y=y, gate=g2, shift=sh1n, scale=sc1n, m_rows=m_out)
        else:
            (xa,) = resmod(xa, n_lat, y=y, gate=g2, m_rows=m_out)
    return xa[:n_lat][None]
```

```python
import functools
import math

import jax
import jax.numpy as jnp
from jax import lax
from jax.experimental import pallas as pl
from jax.experimental.pallas import tpu as pltpu

GRID_W = 64
EPS = 1e-6
ROPE_BASE = 10000.0
GLA_HEADS = 8
GLA_TAU = 16.0
MLA_NOPE = 128
MLA_ROPE = 64

LANES = 128
SUBLANES_BF16 = 16
VMEM_LIMIT_BYTES = 56 * 1024 * 1024

ROW_TILE = 256
GLA_CHUNK = 64
GLA_SUB = 16
SEQ_BLOCK = 256

_HIGHEST = lax.Precision.HIGHEST
_F32 = jnp.float32
_BF16 = jnp.bfloat16


def _pick_tile(n, cap, align):
    best = None
    d = align
    while d <= min(n, cap):
        if n % d == 0:
            best = d
        d += align
    return best if best is not None else n


def _params(sem, vmem=None):
    return pltpu.CompilerParams(dimension_semantics=sem, vmem_limit_bytes=vmem or VMEM_LIMIT_BYTES)


def _silu(x):
    return x * (1.0 / (1.0 + jnp.exp(-x)))


def _mm_fullk_kernel(a_ref, b_ref, o_ref):
    o_ref[...] = jnp.dot(a_ref[...], b_ref[...], preferred_element_type=_F32).astype(o_ref.dtype)


def _mm_ksplit_kernel(a_ref, b_ref, o_ref, acc_ref, *, nk):
    k = pl.program_id(2)

    @pl.when(k == 0)
    def _():
        acc_ref[...] = jnp.zeros_like(acc_ref)

    acc_ref[...] += jnp.dot(a_ref[...], b_ref[...], preferred_element_type=_F32)

    @pl.when(k == nk - 1)
    def _():
        o_ref[...] = acc_ref[...].astype(o_ref.dtype)


def mm(a, w, layer, out_dtype, m_rows=None, col0=0, n_cols=None, name="mm"):
    m = a.shape[0] if m_rows is None else m_rows
    _, k, n_all = w.shape
    n = n_all - col0 if n_cols is None else n_cols
    assert a.shape[1] == k and a.dtype == _BF16 and w.dtype == _BF16
    tm = _pick_tile(m, 1056, SUBLANES_BF16)
    tn = _pick_tile(n, 1024, LANES)
    if n > 1024 and tn < 512:
        tn = n
    assert col0 % tn == 0 and n % tn == 0
    j0 = col0 // tn
    tk = k if k <= 4096 else _pick_tile(k, 2816, LANES)
    out_shape = jax.ShapeDtypeStruct((m, n), out_dtype)
    if tk == k:
        return pl.pallas_call(
            _mm_fullk_kernel,
            grid=(m // tm, n // tn),
            in_specs=[pl.BlockSpec((tm, k), lambda i, j: (i, 0)),
                      pl.BlockSpec((None, k, tn), lambda i, j: (layer, 0, j0 + j))],
            out_specs=pl.BlockSpec((tm, tn), lambda i, j: (i, j)),
            out_shape=out_shape,
            compiler_params=_params(("parallel", "arbitrary")),
            name=name,
        )(a, w)
    nk = k // tk
    return pl.pallas_call(
        functools.partial(_mm_ksplit_kernel, nk=nk),
        grid=(m // tm, n // tn, nk),
        in_specs=[pl.BlockSpec((tm, tk), lambda i, j, kk: (i, kk)),
                  pl.BlockSpec((None, tk, tn), lambda i, j, kk: (layer, kk, j0 + j))],
        out_specs=pl.BlockSpec((tm, tn), lambda i, j, kk: (i, j)),
        out_shape=out_shape,
        scratch_shapes=[pltpu.VMEM((tm, tn), _F32)],
        compiler_params=_params(("parallel", "arbitrary", "arbitrary")),
        name=name,
    )(a, w)


def _ffn_in_kernel(a_ref, bg_ref, bu_ref, g_ref, u_ref, *, n_real):
    j = pl.program_id(1)

    @pl.when(j < n_real)
    def _():
        a = a_ref[...]
        g_ref[...] = jnp.dot(a, bg_ref[...], preferred_element_type=_F32).astype(g_ref.dtype)
        u_ref[...] = jnp.dot(a, bu_ref[...], preferred_element_type=_F32).astype(u_ref.dtype)

    @pl.when(j >= n_real)
    def _():
        g_ref[...] = jnp.zeros_like(g_ref)
        u_ref[...] = jnp.zeros_like(u_ref)


def ffn_in(h, w, layer, d_ff, f_pad, m_rows):
    _, k, n2 = w.shape
    tn = 2 * LANES
    assert n2 == 2 * d_ff and d_ff % tn == 0 and f_pad % tn == 0 and h.dtype == _BF16 and w.dtype == _BF16
    tm = _pick_tile(m_rows, 2112, SUBLANES_BF16)
    n_real = d_ff // tn
    out = pl.BlockSpec((tm, tn), lambda i, j: (i, j))
    shape = jax.ShapeDtypeStruct((m_rows, f_pad), _BF16)
    return pl.pallas_call(
        functools.partial(_ffn_in_kernel, n_real=n_real),
        grid=(m_rows // tm, f_pad // tn),
        in_specs=[pl.BlockSpec((tm, k), lambda i, j: (i, 0)),
                  pl.BlockSpec((None, k, tn), lambda i, j: (layer, 0, jnp.minimum(j, n_real - 1))),
                  pl.BlockSpec((None, k, tn), lambda i, j: (layer, 0, n_real + jnp.minimum(j, n_real - 1)))],
        out_specs=[out, out],
        out_shape=[shape, shape],
        compiler_params=_params(("parallel", "arbitrary")),
        name="ffn_in",
    )(h, w, w)


def _ada_kernel(cond_ref, down_ref, up_ref, o_ref, t_ref):
    @pl.when(pl.program_id(1) == 0)
    def _():
        t_ref[...] = jnp.dot(_silu(cond_ref[...]), down_ref[...], precision=_HIGHEST,
                             preferred_element_type=_F32)

    o_ref[...] = jnp.dot(t_ref[...], up_ref[...], precision=_HIGHEST, preferred_element_type=_F32)


def ada_all(cond8, ada_down, ada_up):
    depth, d, r = ada_down.shape
    n = ada_up.shape[-1]
    tn = _pick_tile(n, 2048, LANES)
    return pl.pallas_call(
        _ada_kernel,
        grid=(depth, n // tn),
        in_specs=[pl.BlockSpec((8, d), lambda l, j: (0, 0)),
                  pl.BlockSpec((None, d, r), lambda l, j: (l, 0, 0)),
                  pl.BlockSpec((None, r, tn), lambda l, j: (l, 0, j))],
        out_specs=pl.BlockSpec((None, 8, tn), lambda l, j: (l, 0, j)),
        out_shape=jax.ShapeDtypeStruct((depth, 8, n), _F32),
        scratch_shapes=[pltpu.VMEM((8, r), _F32)],
        compiler_params=_params(("arbitrary", "arbitrary")),
        name="ada",
    )(cond8, ada_down, ada_up)


def _resmod_kernel(*refs, has_y, has_mod):
    it = iter(refs)
    x_ref = next(it)
    if has_y:
        y_ref, g_ref = next(it), next(it)
    if has_mod:
        sh_ref, sc_ref = next(it), next(it)
    if has_y:
        xo_ref = next(it)
    if has_mod:
        h_ref = next(it)
    x = x_ref[...]
    if has_y:
        x = x + g_ref[...] * y_ref[...]
        xo_ref[...] = x
    if has_mod:
        ms = jnp.mean(x * x, axis=-1, keepdims=True)
        h = x * lax.rsqrt(ms + EPS)
        h_ref[...] = (h * (1.0 + sc_ref[...]) + sh_ref[...]).astype(h_ref.dtype)


def resmod(xa, n_lat, y=None, gate=None, shift=None, scale=None, m_rows=None):
    m = xa.shape[0] if m_rows is None else m_rows
    d = xa.shape[1]
    tr = ROW_TILE
    assert m % tr == 0 and n_lat % tr == 0
    has_y, has_mod = y is not None, shift is not None
    n_lat_blocks = n_lat // tr
    row = pl.BlockSpec((tr, d), lambda i: (i, 0))
    vec = pl.BlockSpec((None, 1, d), lambda i: (jnp.where(i >= n_lat_blocks, 1, 0), 0, 0))
    args, in_specs, out_shape, out_specs = [xa], [row], [], []
    if has_y:
        args += [y, gate]
        in_specs += [row, vec]
        out_shape.append(jax.ShapeDtypeStruct((m, d), _F32))
        out_specs.append(row)
    if has_mod:
        args += [shift, scale]
        in_specs += [vec, vec]
        out_shape.append(jax.ShapeDtypeStruct((m, d), _BF16))
        out_specs.append(row)
    outs = pl.pallas_call(
        functools.partial(_resmod_kernel, has_y=has_y, has_mod=has_mod),
        grid=(m // tr,),
        in_specs=in_specs,
        out_specs=out_specs,
        out_shape=out_shape,
        compiler_params=_params(("parallel",)),
        name="resmod",
    )(*args)
    return outs


def _convglu_kernel(g_ref, u_ref, gp_ref, gn_ref, w_ref, b_ref, o_ref, *, tr, n_lat_blocks, n_blocks):
    i = pl.program_id(0)
    g = g_ref[...].astype(_F32)
    hs = SUBLANES_BF16
    starts_seq = jnp.logical_or(i == 0, i == n_lat_blocks)
    ends_seq = jnp.logical_or(i == n_lat_blocks - 1, i == n_blocks - 1)
    prev_row = jnp.where(starts_seq, 0.0, gp_ref[hs - 1:hs, :].astype(_F32))
    next_row = jnp.where(ends_seq, 0.0, gn_ref[0:1, :].astype(_F32))
    rows = lax.broadcasted_iota(jnp.int32, g.shape, 0)
    g_prev = jnp.where(rows == 0, prev_row, pltpu.roll(g, 1, axis=0))
    g_next = jnp.where(rows == tr - 1, next_row, pltpu.roll(g, tr - 1, axis=0))
    w = w_ref[...]
    z = g_prev * w[0:1, :] + g * w[1:2, :] + g_next * w[2:3, :] + b_ref[...]
    c0 = math.sqrt(2.0 / math.pi)
    gelu = 0.5 * z * (1.0 + jnp.tanh(c0 * (z + 0.044715 * (z * z * z))))
    o_ref[...] = (gelu * u_ref[...].astype(_F32)).astype(o_ref.dtype)


def convglu(gate, up, conv_w, conv_b, n_lat):
    m, f_pad = gate.shape
    tr = ROW_TILE
    tc = _pick_tile(f_pad, 1408, LANES)
    hs = SUBLANES_BF16
    n_blocks = m // tr
    last_halo = m // hs - 1
    return pl.pallas_call(
        functools.partial(_convglu_kernel, tr=tr, n_lat_blocks=n_lat // tr, n_blocks=n_blocks),
        grid=(n_blocks, f_pad // tc),
        in_specs=[pl.BlockSpec((tr, tc), lambda i, j: (i, j)),
                  pl.BlockSpec((tr, tc), lambda i, j: (i, j)),
                  pl.BlockSpec((hs, tc), lambda i, j: (jnp.maximum(i * (tr // hs) - 1, 0), j)),
                  pl.BlockSpec((hs, tc), lambda i, j: (jnp.minimum((i + 1) * (tr // hs), last_halo), j)),
                  pl.BlockSpec((8, tc), lambda i, j: (0, j)),
                  pl.BlockSpec((1, tc), lambda i, j: (0, j))],
        out_specs=pl.BlockSpec((tr, tc), lambda i, j: (i, j)),
        out_shape=jax.ShapeDtypeStruct((m, f_pad), _BF16),
        compiler_params=_params(("parallel", "arbitrary")),
        name="convglu",
    )(gate, up, gate, gate, conv_w, conv_b)


def _seq_block_index(c, n_lat_blocks, n_ctx_blocks, reverse):
    if reverse:
        ctx = n_lat_blocks + n_ctx_blocks - 1 - c
        lat = n_lat_blocks - 1 - (c - n_ctx_blocks)
    else:
        ctx = n_lat_blocks + c
        lat = c - n_ctx_blocks
    return jnp.where(c < n_ctx_blocks, ctx, lat)


def _gla_kernel(q_ref, k_ref, v_ref, a_ref, w2_ref, ba_ref, o_ref, s_ref, *, reverse, rank, scale):
    C, SUB, TB = GLA_CHUNK, GLA_SUB, SEQ_BLOCK
    HALF = SUB // 2
    n_chunks = TB // C
    n_sub = C // SUB
    dv = v_ref.shape[1]

    @pl.when(pl.program_id(1) == 0)
    def _():
        s_ref[...] = jnp.zeros_like(s_ref)

    lane0 = rank if reverse else 0
    a = a_ref[:, lane0:lane0 + rank]
    z = jnp.dot(a, w2_ref[...], precision=_HIGHEST, preferred_element_type=_F32) + ba_ref[...]
    la = (jnp.minimum(z, 0.0) - jnp.log1p(jnp.exp(-jnp.abs(z)))) * (1.0 / GLA_TAU)
    t_idx = lax.broadcasted_iota(jnp.int32, (TB, TB), 0)
    u_idx = lax.broadcasted_iota(jnp.int32, (TB, TB), 1)
    same_chunk = (t_idx // C) == (u_idx // C)
    along = (u_idx >= t_idx) if reverse else (u_idx <= t_idx)
    cumsum_op = jnp.where(jnp.logical_and(same_chunk, along), 1.0, 0.0).astype(_F32)
    b_all = jnp.dot(cumsum_op, la, precision=_HIGHEST, preferred_element_type=_F32)
    sel_t = lax.broadcasted_iota(jnp.int32, (TB, n_chunks * LANES), 0) // C
    sel_c = lax.broadcasted_iota(jnp.int32, (TB, n_chunks * LANES), 1) // LANES
    chunk_sel = jnp.where(sel_t == sel_c, 1.0, 0.0).astype(_F32)
    tot_all = lax.dot_general(la, chunk_sel, (((0,), (0,)), ((), ())), precision=_HIGHEST,
                              preferred_element_type=_F32)
    q_all = q_ref[...] * scale
    k_all = k_ref[...]
    key_idx = lax.broadcasted_iota(jnp.int32, (HALF, C), 1)
    half_rows = lax.broadcasted_iota(jnp.int32, (HALF, 1), 0)

    s = s_ref[...]
    for ci in range(n_chunks):
        cidx = (n_chunks - 1 - ci) if reverse else ci
        rows = slice(cidx * C, (cidx + 1) * C)
        b, q, k, v = b_all[rows], q_all[rows], k_all[rows], v_ref[rows, :]
        o = jnp.dot((q * jnp.exp(b)).astype(_BF16), s.astype(_BF16), preferred_element_type=_F32)

        a_rows = []
        for i in range(n_sub):
            lo = i * SUB
            bi, qi, ki = b[lo:lo + SUB], q[lo:lo + SUB], k[lo:lo + SUB]
            has_earlier = (i < n_sub - 1) if reverse else (i > 0)
            if has_earlier:
                ref_row = lo + SUB - 1 if reverse else lo
                bref = b[ref_row:ref_row + 1]
                qe = qi * jnp.exp(bi - bref)
                ke = k * jnp.exp(jnp.minimum(bref - b, 0.0))
                off = lax.dot_general(qe.astype(_BF16), ke.astype(_BF16), (((1,), (1,)), ((), ())),
                                      preferred_element_type=_F32)
                earlier = (key_idx >= lo + SUB) if reverse else (key_idx < lo)
            for hf in range(2):
                h0 = hf * HALF
                bt, qt = bi[h0:h0 + HALF], qi[h0:h0 + HALF]
                ah = jnp.where(earlier, off[h0:h0 + HALF], 0.0) if has_earlier else jnp.zeros((HALF, C), _F32)
                s_range = range(h0, SUB) if reverse else range(0, h0 + HALF)
                for s_i in s_range:
                    dlt = bt - bi[s_i:s_i + 1]
                    if s_i // HALF == hf:
                        seen = (half_rows <= s_i - h0) if reverse else (half_rows >= s_i - h0)
                        dlt = jnp.where(seen, dlt, -jnp.inf)
                    wgt = qt * ki[s_i:s_i + 1] * jnp.exp(dlt)
                    col = jnp.sum(wgt, axis=1, keepdims=True)
                    ah = jnp.where(key_idx == lo + s_i, col, ah)
                a_rows.append(ah)
        amat = jnp.concatenate(a_rows, axis=0)
        o_ref[rows, :] = o + jnp.dot(amat.astype(_BF16), v, preferred_element_type=_F32)

        end_row = 0 if reverse else C - 1
        kd = (k * jnp.exp(b[end_row:end_row + 1] - b)).astype(_BF16)
        dec = jnp.exp(tot_all[:, cidx * LANES:(cidx + 1) * LANES])
        dec = jnp.concatenate([dec] * (dv // LANES), axis=1)
        s = dec * s + lax.dot_general(kd, v, (((0,), (0,)), ((), ())), preferred_element_type=_F32)
    s_ref[...] = s


def gla_core(qk, v, a, w_a2, b_a, n_lat, reverse):
    m = qk.shape[0]
    dk_all = qk.shape[1] // 2
    dv_all = v.shape[1]
    heads = GLA_HEADS
    dkh, dvh = dk_all // heads, dv_all // heads
    rank = w_a2.shape[1]
    tb = SEQ_BLOCK
    nlb, ncb = n_lat // tb, (m - n_lat) // tb
    d = 1 if reverse else 0
    blk = functools.partial(_seq_block_index, n_lat_blocks=nlb, n_ctx_blocks=ncb, reverse=reverse)
    return pl.pallas_call(
        functools.partial(_gla_kernel, reverse=reverse, rank=rank, scale=dkh ** -0.5),
        grid=(heads, nlb + ncb),
        in_specs=[pl.BlockSpec((tb, dkh), lambda h, c: (blk(c), h)),
                  pl.BlockSpec((tb, dkh), lambda h, c: (blk(c), heads + h)),
                  pl.BlockSpec((tb, dvh), lambda h, c: (blk(c), h)),
                  pl.BlockSpec((tb, LANES), lambda h, c: (blk(c), 0)),
                  pl.BlockSpec((None, rank, dkh), lambda h, c: (d, 0, h)),
                  pl.BlockSpec((None, 1, dkh), lambda h, c: (d, 0, h))],
        out_specs=pl.BlockSpec((tb, dvh), lambda h, c: (blk(c), h)),
        out_shape=jax.ShapeDtypeStruct((m, dv_all), _F32),
        scratch_shapes=[pltpu.VMEM((dkh, dvh), _F32)],
        compiler_params=_params(("parallel", "arbitrary")),
        name="gla_rev" if reverse else "gla_fwd",
    )(qk, qk, v, a, w_a2, b_a)


def _gla_gate_kernel(of_ref, ob_ref, r_ref, br_ref, g_ref, o_ref):
    o = of_ref[...] + ob_ref[...]
    o = o * lax.rsqrt(jnp.mean(o * o, axis=-1, keepdims=True) + EPS) * g_ref[...]
    o_ref[...] = (_silu(r_ref[...] + br_ref[...]) * o).astype(o_ref.dtype)


def gla_gate(o_f, o_b, r, b_r, norm_g, m_rows):
    dv_all = o_f.shape[1]
    dvh = norm_g.shape[-1]
    tr = ROW_TILE
    blk = pl.BlockSpec((tr, dvh), lambda i, h: (i, h))
    return pl.pallas_call(
        _gla_gate_kernel,
        grid=(m_rows // tr, dv_all // dvh),
        in_specs=[blk, blk, blk,
                  pl.BlockSpec((1, dvh), lambda i, h: (0, h)),
                  pl.BlockSpec((1, dvh), lambda i, h: (0, 0))],
        out_specs=blk,
        out_shape=jax.ShapeDtypeStruct((m_rows, dv_all), _BF16),
        compiler_params=_params(("parallel", "arbitrary")),
        name="gla_gate",
    )(o_f, o_b, r, b_r, norm_g)


def _ret_kernel(q_ref, k_ref, v_ref, cos_ref, sin_ref, e_ref, o_ref, s_ref, *, reverse, scale):
    C = SEQ_BLOCK
    dk = q_ref.shape[1]
    dv = v_ref.shape[1]
    half = dk // 2

    @pl.when(pl.program_id(1) == 0)
    def _():
        s_ref[...] = jnp.zeros_like(s_ref)

    lg = jnp.log1p(-jnp.exp2(-e_ref[...]))[0:1, 0:1]
    cos, sin = cos_ref[...], sin_ref[...]

    def rope(x):
        x1, x2 = x[:, :half], x[:, half:]
        return jnp.concatenate([x1 * cos - x2 * sin, x1 * sin + x2 * cos], axis=1)

    q = rope(q_ref[...] * scale)
    k = rope(k_ref[...])
    v = v_ref[...]
    t_idx = lax.broadcasted_iota(jnp.int32, (C, C), 0).astype(_F32)
    s_idx = lax.broadcasted_iota(jnp.int32, (C, C), 1).astype(_F32)
    rel = (s_idx - t_idx) if reverse else (t_idx - s_idx)
    decay = jnp.exp(jnp.where(rel >= 0, lg * rel, -jnp.inf))
    pos = lax.broadcasted_iota(jnp.int32, (C, dk), 0).astype(_F32)
    if reverse:
        q_dec, k_dec = jnp.exp(lg * (C - pos)), jnp.exp(lg * pos)
    else:
        q_dec, k_dec = jnp.exp(lg * (pos + 1.0)), jnp.exp(lg * (C - 1.0 - pos))
    s = s_ref[...]
    scores = lax.dot_general(q.astype(_BF16), k.astype(_BF16), (((1,), (1,)), ((), ())),
                             preferred_element_type=_F32) * decay
    o = jnp.dot(scores.astype(_BF16), v, preferred_element_type=_F32)
    o = o + jnp.dot((q * q_dec).astype(_BF16), s.astype(_BF16), preferred_element_type=_F32)
    o_ref[...] = o
    state_dec = jnp.exp(lg * float(C))
    s_ref[...] = state_dec * s + lax.dot_general((k * k_dec).astype(_BF16), v, (((0,), (0,)), ((), ())),
                                                 preferred_element_type=_F32)


def ret_core(qk, v, cos, sin, dec_b, n_lat, heads, reverse):
    m = qk.shape[0]
    dkh = qk.shape[1] // 2 // heads
    dvh = v.shape[1] // heads
    tb = SEQ_BLOCK
    nlb, ncb = n_lat // tb, (m - n_lat) // tb
    d = 1 if reverse else 0
    blk = functools.partial(_seq_block_index, n_lat_blocks=nlb, n_ctx_blocks=ncb, reverse=reverse)
    return pl.pallas_call(
        functools.partial(_ret_kernel, reverse=reverse, scale=dkh ** -0.5),
        grid=(heads, nlb + ncb),
        in_specs=[pl.BlockSpec((tb, dkh), lambda h, c: (blk(c), h)),
                  pl.BlockSpec((tb, dkh), lambda h, c: (blk(c), heads + h)),
                  pl.BlockSpec((tb, dvh), lambda h, c: (blk(c), h)),
                  pl.BlockSpec((tb, dkh // 2), lambda h, c: (blk(c), 0)),
                  pl.BlockSpec((tb, dkh // 2), lambda h, c: (blk(c), 0)),
                  pl.BlockSpec((None, None, 8, LANES), lambda h, c: (d, h, 0, 0))],
        out_specs=pl.BlockSpec((tb, dvh), lambda h, c: (blk(c), h)),
        out_shape=jax.ShapeDtypeStruct((m, v.shape[1]), _F32),
        scratch_shapes=[pltpu.VMEM((dkh, dvh), _F32)],
        compiler_params=_params(("parallel", "arbitrary")),
        name="ret_rev" if reverse else "ret_fwd",
    )(qk, qk, v, cos, sin, dec_b)


def _ret_gate_kernel(of_ref, ob_ref, g_ref, w_ref, b_ref, o_ref):
    o = of_ref[...] + ob_ref[...]
    mu = jnp.mean(o, axis=-1, keepdims=True)
    oc = o - mu
    var = jnp.mean(oc * oc, axis=-1, keepdims=True)
    o = oc * lax.rsqrt(var + EPS) * w_ref[...] + b_ref[...]
    o_ref[...] = (_silu(g_ref[...]) * o).astype(o_ref.dtype)


def ret_gate(o_f, o_b, g, gn_w, gn_b, heads, m_rows):
    dv_all = o_f.shape[1]
    dvh = dv_all // heads
    tr = ROW_TILE
    blk = pl.BlockSpec((tr, dvh), lambda i, h: (i, h))
    vec = pl.BlockSpec((1, dvh), lambda i, h: (0, h))
    return pl.pallas_call(
        _ret_gate_kernel,
        grid=(m_rows // tr, heads),
        in_specs=[blk, blk, blk, vec, vec],
        out_specs=blk,
        out_shape=jax.ShapeDtypeStruct((m_rows, dv_all), _BF16),
        compiler_params=_params(("parallel", "arbitrary")),
        name="ret_gate",
    )(o_f, o_b, g, gn_w, gn_b)


def _mla_norm_kernel(cq_ref, ckv_ref, qn_ref, kvn_ref, oq_ref, okv_ref):
    cq = cq_ref[...]
    oq_ref[...] = (cq * lax.rsqrt(jnp.mean(cq * cq, axis=-1, keepdims=True) + EPS) * qn_ref[...]).astype(oq_ref.dtype)
    ckv = ckv_ref[...]
    okv_ref[...] = (ckv * lax.rsqrt(jnp.mean(ckv * ckv, axis=-1, keepdims=True) + EPS)
                    * kvn_ref[...]).astype(okv_ref.dtype)


def mla_norm(down, q_norm, kv_norm):
    m = down.shape[0]
    qr, kvr = q_norm.shape[-1], kv_norm.shape[-1]
    assert qr % kvr == 0
    tr = ROW_TILE
    return pl.pallas_call(
        _mla_norm_kernel,
        grid=(m // tr,),
        in_specs=[pl.BlockSpec((tr, qr), lambda i: (i, 0)),
                  pl.BlockSpec((tr, kvr), lambda i: (i, qr // kvr)),
                  pl.BlockSpec((1, qr), lambda i: (0, 0)),
                  pl.BlockSpec((1, kvr), lambda i: (0, 0))],
        out_specs=[pl.BlockSpec((tr, qr), lambda i: (i, 0)), pl.BlockSpec((tr, kvr), lambda i: (i, 0))],
        out_shape=[jax.ShapeDtypeStruct((m, qr), _BF16), jax.ShapeDtypeStruct((m, kvr), _BF16)],
        compiler_params=_params(("parallel",)),
        name="mla_norm",
    )(down, down, q_norm, kv_norm)


def _swap_rope_halves(x):
    lane = lax.broadcasted_iota(jnp.int32, x.shape, 1)
    return jnp.where((lane & 32) == 0, pltpu.roll(x, LANES - 32, axis=1), pltpu.roll(x, 32, axis=1))


def _mla_prep_kernel(qn_ref, qr_ref, kn_ref, kr_ref, v_ref, c_ref, s_ref, gqn_ref, gqr_ref, gkn_ref, gkr_ref,
                     oq_ref, ok_ref, ov_ref, *, qk_dim, scale, dv):
    cosv, sinv = c_ref[...], s_ref[...]
    lane = lax.broadcasted_iota(jnp.int32, cosv.shape, 1)
    low = lane < MLA_ROPE
    qr = qr_ref[...]
    kr = kr_ref[...]
    kr_sq = jnp.sum(kr * kr, axis=-1, keepdims=True)
    kr_g = kr * gkr_ref[...]
    kr_rot = kr_g * cosv + _swap_rope_halves(kr_g) * sinv
    for hh in range(2):
        qn = qn_ref[:, hh * MLA_NOPE:(hh + 1) * MLA_NOPE]
        mine = low if hh == 0 else jnp.logical_not(low)
        q_sq = jnp.sum(qn * qn, axis=-1, keepdims=True) + jnp.sum(jnp.where(mine, qr * qr, 0.0), axis=-1, keepdims=True)
        q_rr = lax.rsqrt(q_sq * (1.0 / qk_dim) + EPS) * scale
        qr_g = qr * q_rr * gqr_ref[...]
        qr_rot = qr_g * cosv + _swap_rope_halves(qr_g) * sinv
        if hh == 1:
            qr_rot = pltpu.roll(qr_rot, MLA_ROPE, axis=1)
        q_full = jnp.concatenate([qn * q_rr * gqn_ref[...], jnp.where(low, qr_rot, 0.0)], axis=1)
        oq_ref[hh] = q_full.T.astype(oq_ref.dtype)

        kn = kn_ref[:, hh * MLA_NOPE:(hh + 1) * MLA_NOPE]
        k_sq = jnp.sum(kn * kn, axis=-1, keepdims=True) + kr_sq
        k_rr = lax.rsqrt(k_sq * (1.0 / qk_dim) + EPS)
        ok_ref[hh, :, 0:MLA_NOPE] = (kn * k_rr * gkn_ref[...]).astype(ok_ref.dtype)
        ok_ref[hh, :, MLA_NOPE:2 * MLA_NOPE] = jnp.where(low, kr_rot * k_rr, 0.0).astype(ok_ref.dtype)

        ov_ref[hh, 0:dv, :] = v_ref[:, hh * dv:(hh + 1) * dv].astype(_F32).T.astype(ov_ref.dtype)
        ov_ref[hh, dv:, :] = jnp.ones((ov_ref.shape[1] - dv, ov_ref.shape[2]), ov_ref.dtype)


def mla_prep(qraw, knope, down, kr_block, v, cos2, sin2, q_gain, k_gain, heads):
    m = qraw.shape[0]
    qk_dim = MLA_NOPE + MLA_ROPE
    dv = v.shape[1] // heads
    assert dv == MLA_NOPE
    tr = _pick_tile(m, 1024, LANES)
    nope_blocks = heads * MLA_NOPE // (2 * MLA_NOPE)
    gqn, gkn = q_gain[None, :MLA_NOPE], k_gain[None, :MLA_NOPE]
    gqr = jnp.tile(q_gain[MLA_NOPE:], 2)[None]
    gkr = jnp.tile(k_gain[MLA_NOPE:], 2)[None]
    vec = pl.BlockSpec((1, LANES), lambda i, p: (0, 0))
    dve = dv + SUBLANES_BF16
    return pl.pallas_call(
        functools.partial(_mla_prep_kernel, qk_dim=qk_dim, scale=qk_dim ** -0.5 * math.log2(math.e), dv=dv),
        grid=(m // tr, heads // 2),
        in_specs=[pl.BlockSpec((tr, 2 * MLA_NOPE), lambda i, p: (i, p)),
                  pl.BlockSpec((tr, LANES), lambda i, p: (i, 2 * nope_blocks + p)),
                  pl.BlockSpec((tr, 2 * MLA_NOPE), lambda i, p: (i, p)),
                  pl.BlockSpec((tr, LANES), lambda i, p: (i, kr_block)),
                  pl.BlockSpec((tr, 2 * dv), lambda i, p: (i, p)),
                  pl.BlockSpec((tr, LANES), lambda i, p: (i, 0)),
                  pl.BlockSpec((tr, LANES), lambda i, p: (i, 0)),
                  vec, vec, vec, vec],
        out_specs=[pl.BlockSpec((2, 2 * MLA_NOPE, tr), lambda i, p: (p, 0, i)),
                   pl.BlockSpec((2, tr, 2 * MLA_NOPE), lambda i, p: (p, i, 0)),
                   pl.BlockSpec((2, dve, tr), lambda i, p: (p, 0, i))],
        out_shape=[jax.ShapeDtypeStruct((heads, 2 * MLA_NOPE, m), _BF16),
                   jax.ShapeDtypeStruct((heads, m, 2 * MLA_NOPE), _BF16),
                   jax.ShapeDtypeStruct((heads, dve, m), _BF16)],
        compiler_params=_params(("parallel", "arbitrary")),
        name="mla_prep",
    )(qraw, qraw, knope, down, v, cos2, sin2, gqn, gqr, gkn, gkr)


def _attn_kernel(qt_ref, k_ref, vt_ref, o_ref, acc_ref, *, tkv, n_kv, dv):
    tq = qt_ref.shape[1]
    qt = qt_ref[...]
    acc_ref[...] = jnp.zeros_like(acc_ref)
    m_run = jnp.full((1, tq), -jnp.inf, _F32)
    for j in range(n_kv):
        rows = slice(j * tkv, (j + 1) * tkv)
        s = jnp.dot(k_ref[rows, :], qt, preferred_element_type=_F32)
        m_new = jnp.maximum(m_run, jnp.max(s, axis=0, keepdims=True))
        alpha = jnp.exp2(m_run - m_new)
        p = jnp.exp2(s - m_new).astype(_BF16)
        acc_ref[...] = alpha * acc_ref[...] + jnp.dot(vt_ref[:, rows], p, preferred_element_type=_F32)
        m_run = m_new
    acc = acc_ref[...]
    o_ref[...] = (acc[:dv] / acc[dv:dv + 1]).T.astype(o_ref.dtype)


def attention(qt_hm, k_hm, vt_hm, q_row0, n_q, kv_row0, n_kv_rows):
    heads, dqk, _ = qt_hm.shape
    dve = vt_hm.shape[1]
    dv = dve - SUBLANES_BF16
    tq = _pick_tile(n_q, 2048, LANES)
    tkv = _pick_tile(n_kv_rows, 384, LANES)
    assert q_row0 % tq == 0 and kv_row0 % n_kv_rows == 0
    q0, kv0 = q_row0 // tq, kv_row0 // n_kv_rows
    return pl.pallas_call(
        functools.partial(_attn_kernel, tkv=tkv, n_kv=n_kv_rows // tkv, dv=dv),
        grid=(heads, n_q // tq),
        in_specs=[pl.BlockSpec((None, dqk, tq), lambda h, i: (h, 0, q0 + i)),
                  pl.BlockSpec((None, n_kv_rows, dqk), lambda h, i: (h, kv0, 0)),
                  pl.BlockSpec((None, dve, n_kv_rows), lambda h, i: (h, 0, kv0))],
        out_specs=pl.BlockSpec((tq, dv), lambda h, i: (i, h)),
        out_shape=jax.ShapeDtypeStruct((n_q, heads * dv), _BF16),
        scratch_shapes=[pltpu.VMEM((dve, tq), _F32)],
        compiler_params=_params(("parallel", "arbitrary")),
        name="attn",
    )(qt_hm, k_hm, vt_hm)


def _axial_rope(n_tokens, dim):
    rows = n_tokens // GRID_W
    quarter = dim // 4
    inv_freq = ROPE_BASE ** (-jnp.arange(quarter, dtype=_F32) / quarter)
    row = jnp.repeat(jnp.arange(rows, dtype=_F32), GRID_W)
    col = jnp.tile(jnp.arange(GRID_W, dtype=_F32), rows)
    ang = jnp.concatenate([row[:, None] * inv_freq, col[:, None] * inv_freq], axis=-1)
    return jnp.cos(ang), jnp.sin(ang)


def _with_ctx_identity(cos, sin, n_ctx):
    return (jnp.concatenate([cos, jnp.ones((n_ctx, cos.shape[1]), _F32)], axis=0),
            jnp.concatenate([sin, jnp.zeros((n_ctx, sin.shape[1]), _F32)], axis=0))


def _pad_last(w, width):
    return w if width == w.shape[-1] else jnp.pad(w, [(0, 0)] * (w.ndim - 1) + [(0, width - w.shape[-1])])


def _round_up(n, mult):
    return (n + mult - 1) // mult * mult


def _gla_layer(h, j, w_in, w_a, w_a2, b_a, b_r, norm_g, w_out, n_lat, m_out):
    dk = w_a2.shape[-1]
    dv = b_r.shape[-1]
    rank = w_a2.shape[1]
    assert 2 * rank <= LANES
    qk = mm(h, w_in, j, _F32, col0=0, n_cols=2 * dk, name="gla_qk")
    v = mm(h, w_in, j, _BF16, col0=2 * dk, n_cols=dv, name="gla_v")
    r = mm(h, w_in, j, _F32, m_rows=m_out, col0=2 * dk + dv, n_cols=dv, name="gla_r")
    a = mm(h, w_a, j, _F32, name="gla_a")
    o_f = gla_core(qk, v, a, w_a2, b_a[:, None, :], n_lat, reverse=False)
    o_b = gla_core(qk, v, a, w_a2, b_a[:, None, :], n_lat, reverse=True)
    gated = gla_gate(o_f, o_b, r, b_r[None], norm_g[None], m_out)
    return mm(gated, w_out, j, _F32, name="gla_out")


def _ret_layer(h, j, w_in, decay, gn_w, gn_b, w_out, rope, n_lat, m_out):
    heads = decay.shape[-1]
    dv = gn_w.shape[-1]
    dk = (w_in.shape[-1] - 2 * dv) // 2
    qk = mm(h, w_in, j, _F32, col0=0, n_cols=2 * dk, name="ret_qk")
    v = mm(h, w_in, j, _BF16, col0=2 * dk, n_cols=dv, name="ret_v")
    g = mm(h, w_in, j, _F32, m_rows=m_out, col0=2 * dk + dv, n_cols=dv, name="ret_g")
    dec_b = jnp.broadcast_to(decay.astype(_F32)[:, :, None, None], (2, heads, 8, LANES))
    cos, sin = rope
    o_f = ret_core(qk, v, cos, sin, dec_b, n_lat, heads, reverse=False)
    o_b = ret_core(qk, v, cos, sin, dec_b, n_lat, heads, reverse=True)
    gated = ret_gate(o_f, o_b, g, gn_w[None], gn_b[None], heads, m_out)
    return mm(gated, w_out, j, _F32, name="ret_out")


def _mla_weights(w_down, w_uq, w_ukv, heads, qk_dim):
    n, q_rank, kv_rank = w_uq.shape[0], w_uq.shape[1], w_ukv.shape[1]
    w_uq4 = w_uq.astype(_BF16).reshape(n, q_rank, heads, qk_dim)
    w_uq_p = jnp.concatenate([w_uq4[..., :MLA_NOPE].reshape(n, q_rank, heads * MLA_NOPE),
                              w_uq4[..., MLA_NOPE:].reshape(n, q_rank, heads * MLA_ROPE)], axis=-1)
    w_ukv4 = w_ukv.astype(_BF16).reshape(n, kv_rank, heads, -1)
    w_uk = w_ukv4[..., :MLA_NOPE].reshape(n, kv_rank, heads * MLA_NOPE)
    w_uv = w_ukv4[..., MLA_NOPE:].reshape(n, kv_rank, -1)
    w_dn = _pad_last(w_down.astype(_BF16), _round_up(w_down.shape[-1], LANES))
    return w_dn, w_uq_p, w_uk, w_uv


def _mla_layer(h, j, w_dn, q_norm, kv_norm, w_uq_p, w_uk, w_uv, q_gain, k_gain, w_out, rope2, n_lat, need_ctx):
    m = h.shape[0]
    n_ctx = m - n_lat
    q_rank, kv_rank = q_norm.shape[-1], kv_norm.shape[-1]
    qk_dim = q_gain.shape[-1]
    heads = w_uk.shape[-1] // MLA_NOPE
    assert qk_dim == MLA_NOPE + MLA_ROPE and heads % 2 == 0 and (q_rank + kv_rank) % LANES == 0
    down = mm(h, w_dn, j, _F32, name="mla_down")
    cqn, ckvn = mla_norm(down, q_norm[None], kv_norm[None])
    qraw = mm(cqn, w_uq_p, j, _F32, name="mla_uq")
    knope = mm(ckvn, w_uk, j, _F32, name="mla_uk")
    v = mm(ckvn, w_uv, j, _BF16, name="mla_uv")
    cos2, sin2 = rope2
    qt_hm, k_hm, vt_hm = mla_prep(qraw, knope, down, (q_rank + kv_rank) // LANES, v, cos2, sin2, q_gain, k_gain,
                                  heads)
    o_lat = attention(qt_hm, k_hm, vt_hm, 0, n_lat, 0, m)
    if need_ctx:
        o_ctx = attention(qt_hm, k_hm, vt_hm, n_lat, n_ctx, n_lat, n_ctx)
        o = jnp.concatenate([o_lat, o_ctx], axis=0)
    else:
        o = o_lat
    return mm(o, w_out, j, _F32, name="mla_out")


def kernel(x, c, ctx, c_ctx, ada_down, ada_up, gla_w_in, gla_w_a2, gla_b_a, gla_b_r, gla_norm, gla_w_out, ret_w_in, ret_decay, ret_gn_w, ret_gn_b, ret_w_out, mla_w_down, mla_q_norm, mla_kv_norm, mla_w_uq, mla_w_ukv, mla_q_gain, mla_k_gain, mla_w_out, ffn_w_in, ffn_conv_w, ffn_conv_b, ffn_w_out):
    assert x.shape[0] == 1 and ctx.shape[0] == 1 and c.shape[0] == 1
    n_lat, d = x.shape[1], x.shape[2]
    n_ctx = ctx.shape[1]
    depth = ada_down.shape[0]
    d_ff = ffn_conv_b.shape[-1]
    f_pad = _round_up(d_ff, 1024)

    xa = jnp.concatenate([x[0], ctx[0]], axis=0)
    cond8 = jnp.zeros((8, d), _F32).at[0].set(c[0]).at[1].set(c_ctx)
    mods = ada_all(cond8, ada_down, ada_up)[:, :2].reshape(depth, 2, 6, 1, d)

    ret_dkh = (ret_w_in.shape[-1] - 2 * ret_gn_w.shape[-1]) // 2 // ret_decay.shape[-1]
    ret_rope = _with_ctx_identity(*_axial_rope(n_lat, ret_dkh), n_ctx)
    cos_m, sin_m = _with_ctx_identity(*_axial_rope(n_lat, MLA_ROPE), n_ctx)
    mla_rope = (jnp.tile(cos_m, (1, 4)), jnp.tile(jnp.concatenate([-sin_m, sin_m], axis=1), (1, 2)))

    gla_in_b = gla_w_in.astype(_BF16)
    gla_rank = gla_w_a2.shape[2]
    gla_a_b = _pad_last(gla_w_in[:, :, gla_w_in.shape[-1] - 2 * gla_rank:].astype(_BF16), LANES)
    gla_out_b = gla_w_out.astype(_BF16)
    ret_in_b, ret_out_b = ret_w_in.astype(_BF16), ret_w_out.astype(_BF16)
    if mla_w_uq.shape[0]:
        mla_heads = mla_w_uq.shape[-1] // mla_q_gain.shape[-1]
        mla_dn_b, mla_uq_b, mla_uk_b, mla_uv_b = _mla_weights(mla_w_down, mla_w_uq, mla_w_ukv, mla_heads,
                                                              mla_q_gain.shape[-1])
        mla_out_b = mla_w_out.astype(_BF16)
    ffn_in_b = ffn_w_in.astype(_BF16)
    ffn_out_b = jnp.pad(ffn_w_out.astype(_BF16), ((0, 0), (0, f_pad - d_ff), (0, 0)))
    conv_w8 = jnp.pad(ffn_conv_w, ((0, 0), (0, 8 - ffn_conv_w.shape[1]), (0, f_pad - d_ff)))
    conv_b1 = _pad_last(ffn_conv_b, f_pad)[:, None, :]

    h = None
    for i in range(depth):
        kind, j = i % 3, i // 3
        need_ctx = i < depth - 1
        m_out = n_lat + n_ctx if need_ctx else n_lat
        sh1, sc1, g1, sh2, sc2, g2 = (mods[i, :, t] for t in range(6))
        if i == 0:
            (h,) = resmod(xa, n_lat, shift=sh1, scale=sc1)
        if kind == 0:
            y = _gla_layer(h, j, gla_in_b, gla_a_b, gla_w_a2[j], gla_b_a[j], gla_b_r[j], gla_norm[j], gla_out_b,
                           n_lat, m_out)
        elif kind == 1:
            y = _ret_layer(h, j, ret_in_b, ret_decay[j], ret_gn_w[j], ret_gn_b[j], ret_out_b, ret_rope,
                           n_lat, m_out)
        else:
            y = _mla_layer(h, j, mla_dn_b, mla_q_norm[j], mla_kv_norm[j], mla_uq_b, mla_uk_b, mla_uv_b,
                           mla_q_gain[j], mla_k_gain[j], mla_out_b, mla_rope, n_lat, need_ctx)
        xa, h = resmod(xa, n_lat, y=y, gate=g1, shift=sh2, scale=sc2, m_rows=m_out)

        gate, up = ffn_in(h, ffn_in_b, i, d_ff, f_pad, m_out)
        act = convglu(gate, up, conv_w8[i], conv_b1[i], n_lat if need_ctx else m_out)
        y = mm(act, ffn_out_b, i, _F32, name="ffn_out")
        if i + 1 < depth:
            sh1n, sc1n = mods[i + 1, :, 0], mods[i + 1, :, 1]
            xa, h = resmod(xa, n_lat, y=y, gate=g2, shift=sh1n, scale=sc1n, m_rows=m_out)
        else:
            (xa,) = resmod(xa, n_lat, y=y, gate=g2, m_rows=m_out)
    return xa[:n_lat][None]
```

```python
import functools
import math

import jax
import jax.numpy as jnp
from jax import lax
from jax.experimental import pallas as pl
from jax.experimental.pallas import tpu as pltpu

GRID_W = 64
EPS = 1e-6
ROPE_BASE = 10000.0
GLA_HEADS = 8
GLA_TAU = 16.0
MLA_NOPE = 128
MLA_ROPE = 64

LANES = 128
SUBLANES_BF16 = 16
VMEM_LIMIT_BYTES = 56 * 1024 * 1024

ROW_TILE = 256
GLA_CHUNK = 64
GLA_SUB = 16
SEQ_BLOCK = 256

_HIGHEST = lax.Precision.HIGHEST
_F32 = jnp.float32
_BF16 = jnp.bfloat16


def _pick_tile(n, cap, align):
    best = None
    d = align
    while d <= min(n, cap):
        if n % d == 0:
            best = d
        d += align
    return best if best is not None else n


def _params(sem, vmem=None):
    return pltpu.CompilerParams(dimension_semantics=sem, vmem_limit_bytes=vmem or VMEM_LIMIT_BYTES)


def _silu(x):
    return x * (1.0 / (1.0 + jnp.exp(-x)))


def _mm_fullk_kernel(a_ref, b_ref, o_ref):
    o_ref[...] = jnp.dot(a_ref[...], b_ref[...], preferred_element_type=_F32).astype(o_ref.dtype)


def _mm_ksplit_kernel(a_ref, b_ref, o_ref, acc_ref, *, nk):
    k = pl.program_id(2)

    @pl.when(k == 0)
    def _():
        acc_ref[...] = jnp.zeros_like(acc_ref)

    acc_ref[...] += jnp.dot(a_ref[...], b_ref[...], preferred_element_type=_F32)

    @pl.when(k == nk - 1)
    def _():
        o_ref[...] = acc_ref[...].astype(o_ref.dtype)


def mm(a, w, layer, out_dtype, m_rows=None, col0=0, n_cols=None, name="mm"):
    m = a.shape[0] if m_rows is None else m_rows
    _, k, n_all = w.shape
    n = n_all - col0 if n_cols is None else n_cols
    assert a.shape[1] == k and a.dtype == _BF16 and w.dtype == _BF16
    tm = _pick_tile(m, 1056, SUBLANES_BF16)
    tn = _pick_tile(n, 1024, LANES)
    if n > 1024 and tn < 512:
        tn = n
    assert col0 % tn == 0 and n % tn == 0
    j0 = col0 // tn
    tk = k if k <= 4096 else _pick_tile(k, 2816, LANES)
    out_shape = jax.ShapeDtypeStruct((m, n), out_dtype)
    if tk == k:
        return pl.pallas_call(
            _mm_fullk_kernel,
            grid=(m // tm, n // tn),
            in_specs=[pl.BlockSpec((tm, k), lambda i, j: (i, 0)),
                      pl.BlockSpec((None, k, tn), lambda i, j: (layer, 0, j0 + j))],
            out_specs=pl.BlockSpec((tm, tn), lambda i, j: (i, j)),
            out_shape=out_shape,
            compiler_params=_params(("parallel", "arbitrary")),
            name=name,
        )(a, w)
    nk = k // tk
    return pl.pallas_call(
        functools.partial(_mm_ksplit_kernel, nk=nk),
        grid=(m // tm, n // tn, nk),
        in_specs=[pl.BlockSpec((tm, tk), lambda i, j, kk: (i, kk)),
                  pl.BlockSpec((None, tk, tn), lambda i, j, kk: (layer, kk, j0 + j))],
        out_specs=pl.BlockSpec((tm, tn), lambda i, j, kk: (i, j)),
        out_shape=out_shape,
        scratch_shapes=[pltpu.VMEM((tm, tn), _F32)],
        compiler_params=_params(("parallel", "arbitrary", "arbitrary")),
        name=name,
    )(a, w)


def _ffn_in_kernel(a_ref, bg_ref, bu_ref, g_ref, u_ref, *, n_real):
    j = pl.program_id(1)

    @pl.when(j < n_real)
    def _():
        a = a_ref[...]
        g_ref[...] = jnp.dot(a, bg_ref[...], preferred_element_type=_F32).astype(g_ref.dtype)
        u_ref[...] = jnp.dot(a, bu_ref[...], preferred_element_type=_F32).astype(u_ref.dtype)

    @pl.when(j >= n_real)
    def _():
        g_ref[...] = jnp.zeros_like(g_ref)
        u_ref[...] = jnp.zeros_like(u_ref)


def ffn_in(h, w, layer, d_ff, f_pad, m_rows):
    _, k, n2 = w.shape
    tn = 2 * LANES
    assert n2 == 2 * d_ff and d_ff % tn == 0 and f_pad % tn == 0 and h.dtype == _BF16 and w.dtype == _BF16
    tm = _pick_tile(m_rows, 2112, SUBLANES_BF16)
    n_real = d_ff // tn
    out = pl.BlockSpec((tm, tn), lambda i, j: (i, j))
    shape = jax.ShapeDtypeStruct((m_rows, f_pad), _BF16)
    return pl.pallas_call(
        functools.partial(_ffn_in_kernel, n_real=n_real),
        grid=(m_rows // tm, f_pad // tn),
        in_specs=[pl.BlockSpec((tm, k), lambda i, j: (i, 0)),
                  pl.BlockSpec((None, k, tn), lambda i, j: (layer, 0, jnp.minimum(j, n_real - 1))),
                  pl.BlockSpec((None, k, tn), lambda i, j: (layer, 0, n_real + jnp.minimum(j, n_real - 1)))],
        out_specs=[out, out],
        out_shape=[shape, shape],
        compiler_params=_params(("parallel", "arbitrary")),
        name="ffn_in",
    )(h, w, w)


def _ada_kernel(cond_ref, down_ref, up_ref, o_ref, t_ref):
    @pl.when(pl.program_id(1) == 0)
    def _():
        t_ref[...] = jnp.dot(_silu(cond_ref[...]), down_ref[...], precision=_HIGHEST,
                             preferred_element_type=_F32)

    o_ref[...] = jnp.dot(t_ref[...], up_ref[...], precision=_HIGHEST, preferred_element_type=_F32)


def ada_all(cond8, ada_down, ada_up):
    depth, d, r = ada_down.shape
    n = ada_up.shape[-1]
    tn = _pick_tile(n, 2048, LANES)
    return pl.pallas_call(
        _ada_kernel,
        grid=(depth, n // tn),
        in_specs=[pl.BlockSpec((8, d), lambda l, j: (0, 0)),
                  pl.BlockSpec((None, d, r), lambda l, j: (l, 0, 0)),
                  pl.BlockSpec((None, r, tn), lambda l, j: (l, 0, j))],
        out_specs=pl.BlockSpec((None, 8, tn), lambda l, j: (l, 0, j)),
        out_shape=jax.ShapeDtypeStruct((depth, 8, n), _F32),
        scratch_shapes=[pltpu.VMEM((8, r), _F32)],
        compiler_params=_params(("arbitrary", "arbitrary")),
        name="ada",
    )(cond8, ada_down, ada_up)


def _resmod_kernel(*refs, has_y, has_mod):
    it = iter(refs)
    x_ref = next(it)
    if has_y:
        y_ref, g_ref = next(it), next(it)
    if has_mod:
        sh_ref, sc_ref = next(it), next(it)
    if has_y:
        xo_ref = next(it)
    if has_mod:
        h_ref = next(it)
    x = x_ref[...]
    if has_y:
        x = x + g_ref[...] * y_ref[...]
        xo_ref[...] = x
    if has_mod:
        ms = jnp.mean(x * x, axis=-1, keepdims=True)
        h = x * lax.rsqrt(ms + EPS)
        h_ref[...] = (h * (1.0 + sc_ref[...]) + sh_ref[...]).astype(h_ref.dtype)


def resmod(xa, n_lat, y=None, gate=None, shift=None, scale=None, m_rows=None):
    m = xa.shape[0] if m_rows is None else m_rows
    d = xa.shape[1]
    tr = ROW_TILE
    assert m % tr == 0 and n_lat % tr == 0
    has_y, has_mod = y is not None, shift is not None
    n_lat_blocks = n_lat // tr
    row = pl.BlockSpec((tr, d), lambda i: (i, 0))
    vec = pl.BlockSpec((None, 1, d), lambda i: (jnp.where(i >= n_lat_blocks, 1, 0), 0, 0))
    args, in_specs, out_shape, out_specs = [xa], [row], [], []
    if has_y:
        args += [y, gate]
        in_specs += [row, vec]
        out_shape.append(jax.ShapeDtypeStruct((m, d), _F32))
        out_specs.append(row)
    if has_mod:
        args += [shift, scale]
        in_specs += [vec, vec]
        out_shape.append(jax.ShapeDtypeStruct((m, d), _BF16))
        out_specs.append(row)
    outs = pl.pallas_call(
        functools.partial(_resmod_kernel, has_y=has_y, has_mod=has_mod),
        grid=(m // tr,),
        in_specs=in_specs,
        out_specs=out_specs,
        out_shape=out_shape,
        compiler_params=_params(("parallel",)),
        name="resmod",
    )(*args)
    return outs


def _convglu_kernel(g_ref, u_ref, gp_ref, gn_ref, w_ref, b_ref, o_ref, *, tr, n_lat_blocks, n_blocks):
    i = pl.program_id(0)
    g = g_ref[...].astype(_F32)
    hs = SUBLANES_BF16
    starts_seq = jnp.logical_or(i == 0, i == n_lat_blocks)
    ends_seq = jnp.logical_or(i == n_lat_blocks - 1, i == n_blocks - 1)
    prev_row = jnp.where(starts_seq, 0.0, gp_ref[hs - 1:hs, :].astype(_F32))
    next_row = jnp.where(ends_seq, 0.0, gn_ref[0:1, :].astype(_F32))
    slab = lax.broadcasted_iota(jnp.int32, (8, g.shape[1]), 0)
    g_dn = pltpu.roll(g, 1, axis=0)
    g_up = pltpu.roll(g, tr - 1, axis=0)
    g_prev = jnp.concatenate([jnp.where(slab == 0, prev_row, g_dn[0:8]), g_dn[8:]], axis=0)
    g_next = jnp.concatenate([g_up[:tr - 8], jnp.where(slab == 7, next_row, g_up[tr - 8:])], axis=0)
    w = w_ref[...]
    z = g_prev * w[0:1, :] + g * w[1:2, :] + g_next * w[2:3, :] + b_ref[...]
    c0 = math.sqrt(2.0 / math.pi)
    t = jnp.tanh(z * (c0 + (c0 * 0.044715) * (z * z)))
    hz = 0.5 * z
    o_ref[...] = ((hz + hz * t) * u_ref[...].astype(_F32)).astype(o_ref.dtype)


def convglu(gate, up, conv_w, conv_b, n_lat):
    m, f_pad = gate.shape
    tr = ROW_TILE
    tc = _pick_tile(f_pad, 1408, LANES)
    hs = SUBLANES_BF16
    n_blocks = m // tr
    last_halo = m // hs - 1
    return pl.pallas_call(
        functools.partial(_convglu_kernel, tr=tr, n_lat_blocks=n_lat // tr, n_blocks=n_blocks),
        grid=(n_blocks, f_pad // tc),
        in_specs=[pl.BlockSpec((tr, tc), lambda i, j: (i, j)),
                  pl.BlockSpec((tr, tc), lambda i, j: (i, j)),
                  pl.BlockSpec((hs, tc), lambda i, j: (jnp.maximum(i * (tr // hs) - 1, 0), j)),
                  pl.BlockSpec((hs, tc), lambda i, j: (jnp.minimum((i + 1) * (tr // hs), last_halo), j)),
                  pl.BlockSpec((8, tc), lambda i, j: (0, j)),
                  pl.BlockSpec((1, tc), lambda i, j: (0, j))],
        out_specs=pl.BlockSpec((tr, tc), lambda i, j: (i, j)),
        out_shape=jax.ShapeDtypeStruct((m, f_pad), _BF16),
        compiler_params=_params(("parallel", "arbitrary")),
        name="convglu",
    )(gate, up, gate, gate, conv_w, conv_b)


def _seq_block_index(c, n_lat_blocks, n_ctx_blocks, reverse):
    if reverse:
        ctx = n_lat_blocks + n_ctx_blocks - 1 - c
        lat = n_lat_blocks - 1 - (c - n_ctx_blocks)
    else:
        ctx = n_lat_blocks + c
        lat = c - n_ctx_blocks
    return jnp.where(c < n_ctx_blocks, ctx, lat)


def _gla_kernel(q_ref, k_ref, v_ref, a_ref, w2_ref, ba_ref, o_ref, s_ref, *, reverse, rank, scale):
    C, SUB, TB = GLA_CHUNK, GLA_SUB, SEQ_BLOCK
    HALF = SUB // 2
    n_chunks = TB // C
    n_sub = C // SUB
    dv = v_ref.shape[1]

    @pl.when(pl.program_id(1) == 0)
    def _():
        s_ref[...] = jnp.zeros_like(s_ref)

    lane0 = rank if reverse else 0
    a = a_ref[:, lane0:lane0 + rank]
    z = jnp.dot(a, w2_ref[...], precision=_HIGHEST, preferred_element_type=_F32) + ba_ref[...]
    la = (jnp.minimum(z, 0.0) - jnp.log1p(jnp.exp(-jnp.abs(z)))) * (1.0 / GLA_TAU)
    t_idx = lax.broadcasted_iota(jnp.int32, (TB, TB), 0)
    u_idx = lax.broadcasted_iota(jnp.int32, (TB, TB), 1)
    same_chunk = (t_idx // C) == (u_idx // C)
    along = (u_idx >= t_idx) if reverse else (u_idx <= t_idx)
    cumsum_op = jnp.where(jnp.logical_and(same_chunk, along), 1.0, 0.0).astype(_F32)
    b_all = jnp.dot(cumsum_op, la, precision=_HIGHEST, preferred_element_type=_F32)
    sel_t = lax.broadcasted_iota(jnp.int32, (TB, n_chunks * LANES), 0) // C
    sel_c = lax.broadcasted_iota(jnp.int32, (TB, n_chunks * LANES), 1) // LANES
    chunk_sel = jnp.where(sel_t == sel_c, 1.0, 0.0).astype(_F32)
    tot_all = lax.dot_general(la, chunk_sel, (((0,), (0,)), ((), ())), precision=_HIGHEST,
                              preferred_element_type=_F32)
    q_all = q_ref[...] * scale
    k_all = k_ref[...]
    key_idx = lax.broadcasted_iota(jnp.int32, (HALF, C), 1)
    half_rows = lax.broadcasted_iota(jnp.int32, (HALF, 1), 0)

    s = s_ref[...]
    for ci in range(n_chunks):
        cidx = (n_chunks - 1 - ci) if reverse else ci
        rows = slice(cidx * C, (cidx + 1) * C)
        b, q, k, v = b_all[rows], q_all[rows], k_all[rows], v_ref[rows, :]
        o = jnp.dot((q * jnp.exp(b)).astype(_BF16), s.astype(_BF16), preferred_element_type=_F32)

        a_rows = []
        for i in range(n_sub):
            lo = i * SUB
            bi, qi, ki = b[lo:lo + SUB], q[lo:lo + SUB], k[lo:lo + SUB]
            has_earlier = (i < n_sub - 1) if reverse else (i > 0)
            if has_earlier:
                ref_row = lo + SUB - 1 if reverse else lo
                bref = b[ref_row:ref_row + 1]
                qe = qi * jnp.exp(bi - bref)
                ke = k * jnp.exp(jnp.minimum(bref - b, 0.0))
                off = lax.dot_general(qe.astype(_BF16), ke.astype(_BF16), (((1,), (1,)), ((), ())),
                                      preferred_element_type=_F32)
                earlier = (key_idx >= lo + SUB) if reverse else (key_idx < lo)
            for hf in range(2):
                h0 = hf * HALF
                bt, qt = bi[h0:h0 + HALF], qi[h0:h0 + HALF]
                ah = jnp.where(earlier, off[h0:h0 + HALF], 0.0) if has_earlier else jnp.zeros((HALF, C), _F32)
                s_range = range(h0, SUB) if reverse else range(0, h0 + HALF)
                for s_i in s_range:
                    dlt = bt - bi[s_i:s_i + 1]
                    if s_i // HALF == hf:
                        seen = (half_rows <= s_i - h0) if reverse else (half_rows >= s_i - h0)
                        dlt = jnp.where(seen, dlt, -jnp.inf)
                    wgt = qt * ki[s_i:s_i + 1] * jnp.exp(dlt)
                    col = jnp.sum(wgt, axis=1, keepdims=True)
                    ah = jnp.where(key_idx == lo + s_i, col, ah)
                a_rows.append(ah)
        amat = jnp.concatenate(a_rows, axis=0)
        o_ref[rows, :] = o + jnp.dot(amat.astype(_BF16), v, preferred_element_type=_F32)

        end_row = 0 if reverse else C - 1
        kd = (k * jnp.exp(b[end_row:end_row + 1] - b)).astype(_BF16)
        dec = jnp.exp(tot_all[:, cidx * LANES:(cidx + 1) * LANES])
        dec = jnp.concatenate([dec] * (dv // LANES), axis=1)
        s = dec * s + lax.dot_general(kd, v, (((0,), (0,)), ((), ())), preferred_element_type=_F32)
    s_ref[...] = s


def gla_core(qk, v, a, w_a2, b_a, n_lat, reverse):
    m = qk.shape[0]
    dk_all = qk.shape[1] // 2
    dv_all = v.shape[1]
    heads = GLA_HEADS
    dkh, dvh = dk_all // heads, dv_all // heads
    rank = w_a2.shape[1]
    tb = SEQ_BLOCK
    nlb, ncb = n_lat // tb, (m - n_lat) // tb
    d = 1 if reverse else 0
    blk = functools.partial(_seq_block_index, n_lat_blocks=nlb, n_ctx_blocks=ncb, reverse=reverse)
    return pl.pallas_call(
        functools.partial(_gla_kernel, reverse=reverse, rank=rank, scale=dkh ** -0.5),
        grid=(heads, nlb + ncb),
        in_specs=[pl.BlockSpec((tb, dkh), lambda h, c: (blk(c), h)),
                  pl.BlockSpec((tb, dkh), lambda h, c: (blk(c), heads + h)),
                  pl.BlockSpec((tb, dvh), lambda h, c: (blk(c), h)),
                  pl.BlockSpec((tb, LANES), lambda h, c: (blk(c), 0)),
                  pl.BlockSpec((None, rank, dkh), lambda h, c: (d, 0, h)),
                  pl.BlockSpec((None, 1, dkh), lambda h, c: (d, 0, h))],
        out_specs=pl.BlockSpec((tb, dvh), lambda h, c: (blk(c), h)),
        out_shape=jax.ShapeDtypeStruct((m, dv_all), _F32),
        scratch_shapes=[pltpu.VMEM((dkh, dvh), _F32)],
        compiler_params=_params(("parallel", "arbitrary")),
        name="gla_rev" if reverse else "gla_fwd",
    )(qk, qk, v, a, w_a2, b_a)


def _gla_gate_kernel(of_ref, ob_ref, r_ref, br_ref, g_ref, o_ref):
    o = of_ref[...] + ob_ref[...]
    o = o * lax.rsqrt(jnp.mean(o * o, axis=-1, keepdims=True) + EPS) * g_ref[...]
    o_ref[...] = (_silu(r_ref[...] + br_ref[...]) * o).astype(o_ref.dtype)


def gla_gate(o_f, o_b, r, b_r, norm_g, m_rows):
    dv_all = o_f.shape[1]
    dvh = norm_g.shape[-1]
    tr = ROW_TILE
    blk = pl.BlockSpec((tr, dvh), lambda i, h: (i, h))
    return pl.pallas_call(
        _gla_gate_kernel,
        grid=(m_rows // tr, dv_all // dvh),
        in_specs=[blk, blk, blk,
                  pl.BlockSpec((1, dvh), lambda i, h: (0, h)),
                  pl.BlockSpec((1, dvh), lambda i, h: (0, 0))],
        out_specs=blk,
        out_shape=jax.ShapeDtypeStruct((m_rows, dv_all), _BF16),
        compiler_params=_params(("parallel", "arbitrary")),
        name="gla_gate",
    )(o_f, o_b, r, b_r, norm_g)


RET_HEADS_PER_STEP = 2


def _ret_kernel(q_ref, k_ref, v_ref, cos_ref, sin_ref, e_ref, o_ref, s_ref, *, reverse, scale, dk, dv):
    C = SEQ_BLOCK
    half = dk // 2

    @pl.when(pl.program_id(1) == 0)
    def _():
        s_ref[...] = jnp.zeros_like(s_ref)

    cos, sin = cos_ref[...], sin_ref[...]

    def rope(x):
        x1, x2 = x[:, :half], x[:, half:]
        return jnp.concatenate([x1 * cos - x2 * sin, x1 * sin + x2 * cos], axis=1)

    t_idx = lax.broadcasted_iota(jnp.int32, (C, C), 0).astype(_F32)
    s_idx = lax.broadcasted_iota(jnp.int32, (C, C), 1).astype(_F32)
    rel = (s_idx - t_idx) if reverse else (t_idx - s_idx)
    pos = lax.broadcasted_iota(jnp.int32, (C, dk), 0).astype(_F32)
    for hh in range(s_ref.shape[0]):
        lg = jnp.log1p(-jnp.exp2(-e_ref[hh]))[0:1, 0:1]
        q = rope(q_ref[:, hh * dk:(hh + 1) * dk] * scale)
        k = rope(k_ref[:, hh * dk:(hh + 1) * dk])
        v = v_ref[:, hh * dv:(hh + 1) * dv]
        decay = jnp.exp(jnp.where(rel >= 0, lg * rel, -jnp.inf))
        if reverse:
            q_dec, k_dec = jnp.exp(lg * (C - pos)), jnp.exp(lg * pos)
        else:
            q_dec, k_dec = jnp.exp(lg * (pos + 1.0)), jnp.exp(lg * (C - 1.0 - pos))
        s = s_ref[hh]
        scores = lax.dot_general(q.astype(_BF16), k.astype(_BF16), (((1,), (1,)), ((), ())),
                                 preferred_element_type=_F32) * decay
        o = jnp.dot(scores.astype(_BF16), v, preferred_element_type=_F32)
        o = o + jnp.dot((q * q_dec).astype(_BF16), s.astype(_BF16), preferred_element_type=_F32)
        o_ref[:, hh * dv:(hh + 1) * dv] = o
        state_dec = jnp.exp(lg * float(C))
        s_ref[hh] = state_dec * s + lax.dot_general((k * k_dec).astype(_BF16), v, (((0,), (0,)), ((), ())),
                                                    preferred_element_type=_F32)


def ret_core(qk, v, cos, sin, dec_b, n_lat, heads, reverse):
    m = qk.shape[0]
    dkh = qk.shape[1] // 2 // heads
    dvh = v.shape[1] // heads
    hp = RET_HEADS_PER_STEP
    assert heads % hp == 0
    groups = heads // hp
    tb = SEQ_BLOCK
    nlb, ncb = n_lat // tb, (m - n_lat) // tb
    d = 1 if reverse else 0
    blk = functools.partial(_seq_block_index, n_lat_blocks=nlb, n_ctx_blocks=ncb, reverse=reverse)
    return pl.pallas_call(
        functools.partial(_ret_kernel, reverse=reverse, scale=dkh ** -0.5, dk=dkh, dv=dvh),
        grid=(groups, nlb + ncb),
        in_specs=[pl.BlockSpec((tb, hp * dkh), lambda h, c: (blk(c), h)),
                  pl.BlockSpec((tb, hp * dkh), lambda h, c: (blk(c), groups + h)),
                  pl.BlockSpec((tb, hp * dvh), lambda h, c: (blk(c), h)),
                  pl.BlockSpec((tb, dkh // 2), lambda h, c: (blk(c), 0)),
                  pl.BlockSpec((tb, dkh // 2), lambda h, c: (blk(c), 0)),
                  pl.BlockSpec((None, hp, 8, LANES), lambda h, c: (d, h, 0, 0))],
        out_specs=pl.BlockSpec((tb, hp * dvh), lambda h, c: (blk(c), h)),
        out_shape=jax.ShapeDtypeStruct((m, v.shape[1]), _F32),
        scratch_shapes=[pltpu.VMEM((hp, dkh, dvh), _F32)],
        compiler_params=_params(("parallel", "arbitrary")),
        name="ret_rev" if reverse else "ret_fwd",
    )(qk, qk, v, cos, sin, dec_b)


def _ret_gate_kernel(of_ref, ob_ref, g_ref, w_ref, b_ref, o_ref):
    o = of_ref[...] + ob_ref[...]
    mu = jnp.mean(o, axis=-1, keepdims=True)
    oc = o - mu
    var = jnp.mean(oc * oc, axis=-1, keepdims=True)
    o = oc * lax.rsqrt(var + EPS) * w_ref[...] + b_ref[...]
    o_ref[...] = (_silu(g_ref[...]) * o).astype(o_ref.dtype)


def ret_gate(o_f, o_b, g, gn_w, gn_b, heads, m_rows):
    dv_all = o_f.shape[1]
    dvh = dv_all // heads
    tr = ROW_TILE
    blk = pl.BlockSpec((tr, dvh), lambda i, h: (i, h))
    vec = pl.BlockSpec((1, dvh), lambda i, h: (0, h))
    return pl.pallas_call(
        _ret_gate_kernel,
        grid=(m_rows // tr, heads),
        in_specs=[blk, blk, blk, vec, vec],
        out_specs=blk,
        out_shape=jax.ShapeDtypeStruct((m_rows, dv_all), _BF16),
        compiler_params=_params(("parallel", "arbitrary")),
        name="ret_gate",
    )(o_f, o_b, g, gn_w, gn_b)


def _mla_norm_kernel(cq_ref, ckv_ref, qn_ref, kvn_ref, oq_ref, okv_ref):
    cq = cq_ref[...]
    oq_ref[...] = (cq * lax.rsqrt(jnp.mean(cq * cq, axis=-1, keepdims=True) + EPS) * qn_ref[...]).astype(oq_ref.dtype)
    ckv = ckv_ref[...]
    okv_ref[...] = (ckv * lax.rsqrt(jnp.mean(ckv * ckv, axis=-1, keepdims=True) + EPS)
                    * kvn_ref[...]).astype(okv_ref.dtype)


def mla_norm(down, q_norm, kv_norm):
    m = down.shape[0]
    qr, kvr = q_norm.shape[-1], kv_norm.shape[-1]
    assert qr % kvr == 0
    tr = ROW_TILE
    return pl.pallas_call(
        _mla_norm_kernel,
        grid=(m // tr,),
        in_specs=[pl.BlockSpec((tr, qr), lambda i: (i, 0)),
                  pl.BlockSpec((tr, kvr), lambda i: (i, qr // kvr)),
                  pl.BlockSpec((1, qr), lambda i: (0, 0)),
                  pl.BlockSpec((1, kvr), lambda i: (0, 0))],
        out_specs=[pl.BlockSpec((tr, qr), lambda i: (i, 0)), pl.BlockSpec((tr, kvr), lambda i: (i, 0))],
        out_shape=[jax.ShapeDtypeStruct((m, qr), _BF16), jax.ShapeDtypeStruct((m, kvr), _BF16)],
        compiler_params=_params(("parallel",)),
        name="mla_norm",
    )(down, down, q_norm, kv_norm)


def _swap_rope_halves(x):
    lane = lax.broadcasted_iota(jnp.int32, x.shape, 1)
    return jnp.where((lane & 32) == 0, pltpu.roll(x, LANES - 32, axis=1), pltpu.roll(x, 32, axis=1))


def _mla_prep_kernel(qn_ref, qr_ref, kn_ref, kr_ref, v_ref, c_ref, s_ref, gqn_ref, gqr_ref, gkn_ref, gkr_ref,
                     oq_ref, ok_ref, ov_ref, *, qk_dim, scale, dv):
    cosv, sinv = c_ref[...], s_ref[...]
    lane = lax.broadcasted_iota(jnp.int32, cosv.shape, 1)
    low = lane < MLA_ROPE
    qr = qr_ref[...]
    kr = kr_ref[...]
    kr_sq = jnp.sum(kr * kr, axis=-1, keepdims=True)
    kr_g = kr * gkr_ref[...]
    kr_rot = kr_g * cosv + _swap_rope_halves(kr_g) * sinv
    for hh in range(2):
        qn = qn_ref[:, hh * MLA_NOPE:(hh + 1) * MLA_NOPE]
        mine = low if hh == 0 else jnp.logical_not(low)
        q_sq = jnp.sum(qn * qn, axis=-1, keepdims=True) + jnp.sum(jnp.where(mine, qr * qr, 0.0), axis=-1, keepdims=True)
        q_rr = lax.rsqrt(q_sq * (1.0 / qk_dim) + EPS) * scale
        qr_g = qr * q_rr * gqr_ref[...]
        qr_rot = qr_g * cosv + _swap_rope_halves(qr_g) * sinv
        if hh == 1:
            qr_rot = pltpu.roll(qr_rot, MLA_ROPE, axis=1)
        q_full = jnp.concatenate([qn * q_rr * gqn_ref[...], jnp.where(low, qr_rot, 0.0)], axis=1)
        oq_ref[hh] = q_full.T.astype(oq_ref.dtype)

        kn = kn_ref[:, hh * MLA_NOPE:(hh + 1) * MLA_NOPE]
        k_sq = jnp.sum(kn * kn, axis=-1, keepdims=True) + kr_sq
        k_rr = lax.rsqrt(k_sq * (1.0 / qk_dim) + EPS)
        ok_ref[hh, :, 0:MLA_NOPE] = (kn * k_rr * gkn_ref[...]).astype(ok_ref.dtype)
        ok_ref[hh, :, MLA_NOPE:2 * MLA_NOPE] = jnp.where(low, kr_rot * k_rr, 0.0).astype(ok_ref.dtype)

        ov_ref[hh, 0:dv, :] = v_ref[:, hh * dv:(hh + 1) * dv].astype(_F32).T.astype(ov_ref.dtype)
        ov_ref[hh, dv:, :] = jnp.ones((ov_ref.shape[1] - dv, ov_ref.shape[2]), ov_ref.dtype)


def mla_prep(qraw, knope, down, kr_block, v, cos2, sin2, q_gain, k_gain, heads):
    m = qraw.shape[0]
    qk_dim = MLA_NOPE + MLA_ROPE
    dv = v.shape[1] // heads
    assert dv == MLA_NOPE
    tr = _pick_tile(m, 1024, LANES)
    nope_blocks = heads * MLA_NOPE // (2 * MLA_NOPE)
    gqn, gkn = q_gain[None, :MLA_NOPE], k_gain[None, :MLA_NOPE]
    gqr = jnp.tile(q_gain[MLA_NOPE:], 2)[None]
    gkr = jnp.tile(k_gain[MLA_NOPE:], 2)[None]
    vec = pl.BlockSpec((1, LANES), lambda i, p: (0, 0))
    dve = dv + SUBLANES_BF16
    return pl.pallas_call(
        functools.partial(_mla_prep_kernel, qk_dim=qk_dim, scale=qk_dim ** -0.5 * math.log2(math.e), dv=dv),
        grid=(m // tr, heads // 2),
        in_specs=[pl.BlockSpec((tr, 2 * MLA_NOPE), lambda i, p: (i, p)),
                  pl.BlockSpec((tr, LANES), lambda i, p: (i, 2 * nope_blocks + p)),
                  pl.BlockSpec((tr, 2 * MLA_NOPE), lambda i, p: (i, p)),
                  pl.BlockSpec((tr, LANES), lambda i, p: (i, kr_block)),
                  pl.BlockSpec((tr, 2 * dv), lambda i, p: (i, p)),
                  pl.BlockSpec((tr, LANES), lambda i, p: (i, 0)),
                  pl.BlockSpec((tr, LANES), lambda i, p: (i, 0)),
                  vec, vec, vec, vec],
        out_specs=[pl.BlockSpec((2, 2 * MLA_NOPE, tr), lambda i, p: (p, 0, i)),
                   pl.BlockSpec((2, tr, 2 * MLA_NOPE), lambda i, p: (p, i, 0)),
                   pl.BlockSpec((2, dve, tr), lambda i, p: (p, 0, i))],
        out_shape=[jax.ShapeDtypeStruct((heads, 2 * MLA_NOPE, m), _BF16),
                   jax.ShapeDtypeStruct((heads, m, 2 * MLA_NOPE), _BF16),
                   jax.ShapeDtypeStruct((heads, dve, m), _BF16)],
        compiler_params=_params(("parallel", "arbitrary")),
        name="mla_prep",
    )(qraw, qraw, knope, down, v, cos2, sin2, gqn, gqr, gkn, gkr)


def _attn_kernel(qt_ref, k_ref, vt_ref, o_ref, acc_ref, *, tkv, n_kv, dv):
    tq = qt_ref.shape[1]
    qt = qt_ref[...]
    acc_ref[...] = jnp.zeros_like(acc_ref)
    m_run = jnp.full((1, tq), -jnp.inf, _F32)
    for j in range(n_kv):
        rows = slice(j * tkv, (j + 1) * tkv)
        s = jnp.dot(k_ref[rows, :], qt, preferred_element_type=_F32)
        m_new = jnp.maximum(m_run, jnp.max(s, axis=0, keepdims=True))
        alpha = jnp.exp2(m_run - m_new)
        p = jnp.exp2(s - m_new).astype(_BF16)
        acc_ref[...] = alpha * acc_ref[...] + jnp.dot(vt_ref[:, rows], p, preferred_element_type=_F32)
        m_run = m_new
    acc = acc_ref[...]
    o_ref[...] = (acc[:dv] / acc[dv:dv + 1]).T.astype(o_ref.dtype)


def attention(qt_hm, k_hm, vt_hm, q_row0, n_q, kv_row0, n_kv_rows):
    heads, dqk, _ = qt_hm.shape
    dve = vt_hm.shape[1]
    dv = dve - SUBLANES_BF16
    tq = _pick_tile(n_q, 2048, LANES)
    tkv = _pick_tile(n_kv_rows, 384, LANES)
    assert q_row0 % tq == 0 and kv_row0 % n_kv_rows == 0
    q0, kv0 = q_row0 // tq, kv_row0 // n_kv_rows
    return pl.pallas_call(
        functools.partial(_attn_kernel, tkv=tkv, n_kv=n_kv_rows // tkv, dv=dv),
        grid=(heads, n_q // tq),
        in_specs=[pl.BlockSpec((None, dqk, tq), lambda h, i: (h, 0, q0 + i)),
                  pl.BlockSpec((None, n_kv_rows, dqk), lambda h, i: (h, kv0, 0)),
                  pl.BlockSpec((None, dve, n_kv_rows), lambda h, i: (h, 0, kv0))],
        out_specs=pl.BlockSpec((tq, dv), lambda h, i: (i, h)),
        out_shape=jax.ShapeDtypeStruct((n_q, heads * dv), _BF16),
        scratch_shapes=[pltpu.VMEM((dve, tq), _F32)],
        compiler_params=_params(("parallel", "arbitrary")),
        name="attn",
    )(qt_hm, k_hm, vt_hm)


def _axial_rope(n_tokens, dim):
    rows = n_tokens // GRID_W
    quarter = dim // 4
    inv_freq = ROPE_BASE ** (-jnp.arange(quarter, dtype=_F32) / quarter)
    row = jnp.repeat(jnp.arange(rows, dtype=_F32), GRID_W)
    col = jnp.tile(jnp.arange(GRID_W, dtype=_F32), rows)
    ang = jnp.concatenate([row[:, None] * inv_freq, col[:, None] * inv_freq], axis=-1)
    return jnp.cos(ang), jnp.sin(ang)


def _with_ctx_identity(cos, sin, n_ctx):
    return (jnp.concatenate([cos, jnp.ones((n_ctx, cos.shape[1]), _F32)], axis=0),
            jnp.concatenate([sin, jnp.zeros((n_ctx, sin.shape[1]), _F32)], axis=0))


def _pad_last(w, width):
    return w if width == w.shape[-1] else jnp.pad(w, [(0, 0)] * (w.ndim - 1) + [(0, width - w.shape[-1])])


def _round_up(n, mult):
    return (n + mult - 1) // mult * mult


def _gla_layer(h, j, w_in, w_a, w_a2, b_a, b_r, norm_g, w_out, n_lat, m_out):
    dk = w_a2.shape[-1]
    dv = b_r.shape[-1]
    rank = w_a2.shape[1]
    assert 2 * rank <= LANES
    qk = mm(h, w_in, j, _F32, col0=0, n_cols=2 * dk, name="gla_qk")
    v = mm(h, w_in, j, _BF16, col0=2 * dk, n_cols=dv, name="gla_v")
    r = mm(h, w_in, j, _F32, m_rows=m_out, col0=2 * dk + dv, n_cols=dv, name="gla_r")
    a = mm(h, w_a, j, _F32, name="gla_a")
    o_f = gla_core(qk, v, a, w_a2, b_a[:, None, :], n_lat, reverse=False)
    o_b = gla_core(qk, v, a, w_a2, b_a[:, None, :], n_lat, reverse=True)
    gated = gla_gate(o_f, o_b, r, b_r[None], norm_g[None], m_out)
    return mm(gated, w_out, j, _F32, name="gla_out")


def _ret_layer(h, j, w_in, decay, gn_w, gn_b, w_out, rope, n_lat, m_out):
    heads = decay.shape[-1]
    dv = gn_w.shape[-1]
    dk = (w_in.shape[-1] - 2 * dv) // 2
    qk = mm(h, w_in, j, _F32, col0=0, n_cols=2 * dk, name="ret_qk")
    v = mm(h, w_in, j, _BF16, col0=2 * dk, n_cols=dv, name="ret_v")
    g = mm(h, w_in, j, _F32, m_rows=m_out, col0=2 * dk + dv, n_cols=dv, name="ret_g")
    dec_b = jnp.broadcast_to(decay.astype(_F32)[:, :, None, None], (2, heads, 8, LANES))
    cos, sin = rope
    o_f = ret_core(qk, v, cos, sin, dec_b, n_lat, heads, reverse=False)
    o_b = ret_core(qk, v, cos, sin, dec_b, n_lat, heads, reverse=True)
    gated = ret_gate(o_f, o_b, g, gn_w[None], gn_b[None], heads, m_out)
    return mm(gated, w_out, j, _F32, name="ret_out")


def _mla_weights(w_down, w_uq, w_ukv, heads, qk_dim):
    n, q_rank, kv_rank = w_uq.shape[0], w_uq.shape[1], w_ukv.shape[1]
    w_uq4 = w_uq.astype(_BF16).reshape(n, q_rank, heads, qk_dim)
    w_uq_p = jnp.concatenate([w_uq4[..., :MLA_NOPE].reshape(n, q_rank, heads * MLA_NOPE),
                              w_uq4[..., MLA_NOPE:].reshape(n, q_rank, heads * MLA_ROPE)], axis=-1)
    w_ukv4 = w_ukv.astype(_BF16).reshape(n, kv_rank, heads, -1)
    w_uk = w_ukv4[..., :MLA_NOPE].reshape(n, kv_rank, heads * MLA_NOPE)
    w_uv = w_ukv4[..., MLA_NOPE:].reshape(n, kv_rank, -1)
    w_dn = _pad_last(w_down.astype(_BF16), _round_up(w_down.shape[-1], LANES))
    return w_dn, w_uq_p, w_uk, w_uv


def _mla_layer(h, j, w_dn, q_norm, kv_norm, w_uq_p, w_uk, w_uv, q_gain, k_gain, w_out, rope2, n_lat, need_ctx):
    m = h.shape[0]
    n_ctx = m - n_lat
    q_rank, kv_rank = q_norm.shape[-1], kv_norm.shape[-1]
    qk_dim = q_gain.shape[-1]
    heads = w_uk.shape[-1] // MLA_NOPE
    assert qk_dim == MLA_NOPE + MLA_ROPE and heads % 2 == 0 and (q_rank + kv_rank) % LANES == 0
    down = mm(h, w_dn, j, _F32, name="mla_down")
    cqn, ckvn = mla_norm(down, q_norm[None], kv_norm[None])
    qraw = mm(cqn, w_uq_p, j, _F32, name="mla_uq")
    knope = mm(ckvn, w_uk, j, _F32, name="mla_uk")
    v = mm(ckvn, w_uv, j, _BF16, name="mla_uv")
    cos2, sin2 = rope2
    qt_hm, k_hm, vt_hm = mla_prep(qraw, knope, down, (q_rank + kv_rank) // LANES, v, cos2, sin2, q_gain, k_gain,
                                  heads)
    o_lat = attention(qt_hm, k_hm, vt_hm, 0, n_lat, 0, m)
    if need_ctx:
        o_ctx = attention(qt_hm, k_hm, vt_hm, n_lat, n_ctx, n_lat, n_ctx)
        o = jnp.concatenate([o_lat, o_ctx], axis=0)
    else:
        o = o_lat
    return mm(o, w_out, j, _F32, name="mla_out")


def kernel(x, c, ctx, c_ctx, ada_down, ada_up, gla_w_in, gla_w_a2, gla_b_a, gla_b_r, gla_norm, gla_w_out, ret_w_in, ret_decay, ret_gn_w, ret_gn_b, ret_w_out, mla_w_down, mla_q_norm, mla_kv_norm, mla_w_uq, mla_w_ukv, mla_q_gain, mla_k_gain, mla_w_out, ffn_w_in, ffn_conv_w, ffn_conv_b, ffn_w_out):
    assert x.shape[0] == 1 and ctx.shape[0] == 1 and c.shape[0] == 1
    n_lat, d = x.shape[1], x.shape[2]
    n_ctx = ctx.shape[1]
    depth = ada_down.shape[0]
    d_ff = ffn_conv_b.shape[-1]
    f_pad = _round_up(d_ff, 1024)

    xa = jnp.concatenate([x[0], ctx[0]], axis=0)
    cond8 = jnp.zeros((8, d), _F32).at[0].set(c[0]).at[1].set(c_ctx)
    mods = ada_all(cond8, ada_down, ada_up)[:, :2].reshape(depth, 2, 6, 1, d)

    ret_dkh = (ret_w_in.shape[-1] - 2 * ret_gn_w.shape[-1]) // 2 // ret_decay.shape[-1]
    ret_rope = _with_ctx_identity(*_axial_rope(n_lat, ret_dkh), n_ctx)
    cos_m, sin_m = _with_ctx_identity(*_axial_rope(n_lat, MLA_ROPE), n_ctx)
    mla_rope = (jnp.tile(cos_m, (1, 4)), jnp.tile(jnp.concatenate([-sin_m, sin_m], axis=1), (1, 2)))

    gla_in_b = gla_w_in.astype(_BF16)
    gla_rank = gla_w_a2.shape[2]
    gla_a_b = _pad_last(gla_w_in[:, :, gla_w_in.shape[-1] - 2 * gla_rank:].astype(_BF16), LANES)
    gla_out_b = gla_w_out.astype(_BF16)
    ret_in_b, ret_out_b = ret_w_in.astype(_BF16), ret_w_out.astype(_BF16)
    if mla_w_uq.shape[0]:
        mla_heads = mla_w_uq.shape[-1] // mla_q_gain.shape[-1]
        mla_dn_b, mla_uq_b, mla_uk_b, mla_uv_b = _mla_weights(mla_w_down, mla_w_uq, mla_w_ukv, mla_heads,
                                                              mla_q_gain.shape[-1])
        mla_out_b = mla_w_out.astype(_BF16)
    ffn_in_b = ffn_w_in.astype(_BF16)
    ffn_out_b = jnp.pad(ffn_w_out.astype(_BF16), ((0, 0), (0, f_pad - d_ff), (0, 0)))
    conv_w8 = jnp.pad(ffn_conv_w, ((0, 0), (0, 8 - ffn_conv_w.shape[1]), (0, f_pad - d_ff)))
    conv_b1 = _pad_last(ffn_conv_b, f_pad)[:, None, :]

    h = None
    for i in range(depth):
        kind, j = i % 3, i // 3
        need_ctx = i < depth - 1
        m_out = n_lat + n_ctx if need_ctx else n_lat
        sh1, sc1, g1, sh2, sc2, g2 = (mods[i, :, t] for t in range(6))
        if i == 0:
            (h,) = resmod(xa, n_lat, shift=sh1, scale=sc1)
        if kind == 0:
            y = _gla_layer(h, j, gla_in_b, gla_a_b, gla_w_a2[j], gla_b_a[j], gla_b_r[j], gla_norm[j], gla_out_b,
                           n_lat, m_out)
        elif kind == 1:
            y = _ret_layer(h, j, ret_in_b, ret_decay[j], ret_gn_w[j], ret_gn_b[j], ret_out_b, ret_rope,
                           n_lat, m_out)
        else:
            y = _mla_layer(h, j, mla_dn_b, mla_q_norm[j], mla_kv_norm[j], mla_uq_b, mla_uk_b, mla_uv_b,
                           mla_q_gain[j], mla_k_gain[j], mla_out_b, mla_rope, n_lat, need_ctx)
        xa, h = resmod(xa, n_lat, y=y, gate=g1, shift=sh2, scale=sc2, m_rows=m_out)

        gate, up = ffn_in(h, ffn_in_b, i, d_ff, f_pad, m_out)
        act = convglu(gate, up, conv_w8[i], conv_b1[i], n_lat if need_ctx else m_out)
        y = mm(act, ffn_out_b, i, _F32, name="ffn_out")
        if i + 1 < depth:
            sh1n, sc1n = mods[i + 1, :, 0], mods[i + 1, :, 1]
            xa, h = resmod(xa, n_lat, y=y, gate=g2, shift=sh1n, scale=sc1n, m_rows=m_out)
        else:
            (xa,) = resmod(xa, n_lat, y=y, gate=g2, m_rows=m_out)
    return xa[:n_lat][None]
```

```python
import functools
import math

import jax
import jax.numpy as jnp
from jax import lax
from jax.experimental import pallas as pl
from jax.experimental.pallas import tpu as pltpu

GRID_W = 64
EPS = 1e-6
ROPE_BASE = 10000.0
GLA_HEADS = 8
GLA_TAU = 16.0
MLA_NOPE = 128
MLA_ROPE = 64

LANES = 128
SUBLANES_BF16 = 16
VMEM_LIMIT_BYTES = 56 * 1024 * 1024

ROW_TILE = 256
GLA_CHUNK = 64
GLA_SUB = 16
SEQ_BLOCK = 256

_HIGHEST = lax.Precision.HIGHEST
_F32 = jnp.float32
_BF16 = jnp.bfloat16


def _pick_tile(n, cap, align):
    best = None
    d = align
    while d <= min(n, cap):
        if n % d == 0:
            best = d
        d += align
    return best if best is not None else n


def _params(sem, vmem=None):
    return pltpu.CompilerParams(dimension_semantics=sem, vmem_limit_bytes=vmem or VMEM_LIMIT_BYTES)


def _silu(x):
    return x * (1.0 / (1.0 + jnp.exp(-x)))


def _mm_fullk_kernel(a_ref, b_ref, o_ref):
    o_ref[...] = jnp.dot(a_ref[...], b_ref[...], preferred_element_type=_F32).astype(o_ref.dtype)


def _mm_ksplit_kernel(a_ref, b_ref, o_ref, acc_ref, *, nk):
    k = pl.program_id(2)

    @pl.when(k == 0)
    def _():
        acc_ref[...] = jnp.zeros_like(acc_ref)

    acc_ref[...] += jnp.dot(a_ref[...], b_ref[...], preferred_element_type=_F32)

    @pl.when(k == nk - 1)
    def _():
        o_ref[...] = acc_ref[...].astype(o_ref.dtype)


def mm(a, w, layer, out_dtype, m_rows=None, col0=0, n_cols=None, name="mm"):
    m = a.shape[0] if m_rows is None else m_rows
    _, k, n_all = w.shape
    n = n_all - col0 if n_cols is None else n_cols
    assert a.shape[1] == k and a.dtype == _BF16 and w.dtype == _BF16
    tm = _pick_tile(m, 1056, SUBLANES_BF16)
    tn = _pick_tile(n, 1024, LANES)
    if n > 1024 and tn < 512:
        tn = n
    assert col0 % tn == 0 and n % tn == 0
    j0 = col0 // tn
    tk = k if k <= 4096 else _pick_tile(k, 2816, LANES)
    out_shape = jax.ShapeDtypeStruct((m, n), out_dtype)
    if tk == k:
        return pl.pallas_call(
            _mm_fullk_kernel,
            grid=(m // tm, n // tn),
            in_specs=[pl.BlockSpec((tm, k), lambda i, j: (i, 0)),
                      pl.BlockSpec((None, k, tn), lambda i, j: (layer, 0, j0 + j))],
            out_specs=pl.BlockSpec((tm, tn), lambda i, j: (i, j)),
            out_shape=out_shape,
            compiler_params=_params(("parallel", "arbitrary")),
            name=name,
        )(a, w)
    nk = k // tk
    return pl.pallas_call(
        functools.partial(_mm_ksplit_kernel, nk=nk),
        grid=(m // tm, n // tn, nk),
        in_specs=[pl.BlockSpec((tm, tk), lambda i, j, kk: (i, kk)),
                  pl.BlockSpec((None, tk, tn), lambda i, j, kk: (layer, kk, j0 + j))],
        out_specs=pl.BlockSpec((tm, tn), lambda i, j, kk: (i, j)),
        out_shape=out_shape,
        scratch_shapes=[pltpu.VMEM((tm, tn), _F32)],
        compiler_params=_params(("parallel", "arbitrary", "arbitrary")),
        name=name,
    )(a, w)


def _ffn_in_kernel(a_ref, bg_ref, bu_ref, g_ref, u_ref, *, n_real):
    j = pl.program_id(1)

    @pl.when(j < n_real)
    def _():
        a = a_ref[...]
        g_ref[...] = jnp.dot(a, bg_ref[...], preferred_element_type=_F32).astype(g_ref.dtype)
        u_ref[...] = jnp.dot(a, bu_ref[...], preferred_element_type=_F32).astype(u_ref.dtype)

    @pl.when(j >= n_real)
    def _():
        g_ref[...] = jnp.zeros_like(g_ref)
        u_ref[...] = jnp.zeros_like(u_ref)


def ffn_in(h, w, layer, d_ff, f_pad, m_rows):
    _, k, n2 = w.shape
    tn = 2 * LANES
    assert n2 == 2 * d_ff and d_ff % tn == 0 and f_pad % tn == 0 and h.dtype == _BF16 and w.dtype == _BF16
    tm = _pick_tile(m_rows, 2112, SUBLANES_BF16)
    n_real = d_ff // tn
    out = pl.BlockSpec((tm, tn), lambda i, j: (i, j))
    shape = jax.ShapeDtypeStruct((m_rows, f_pad), _BF16)
    return pl.pallas_call(
        functools.partial(_ffn_in_kernel, n_real=n_real),
        grid=(m_rows // tm, f_pad // tn),
        in_specs=[pl.BlockSpec((tm, k), lambda i, j: (i, 0)),
                  pl.BlockSpec((None, k, tn), lambda i, j: (layer, 0, jnp.minimum(j, n_real - 1))),
                  pl.BlockSpec((None, k, tn), lambda i, j: (layer, 0, n_real + jnp.minimum(j, n_real - 1)))],
        out_specs=[out, out],
        out_shape=[shape, shape],
        compiler_params=_params(("parallel", "arbitrary")),
        name="ffn_in",
    )(h, w, w)


def _ada_kernel(cond_ref, down_ref, up_ref, o_ref, t_ref):
    @pl.when(pl.program_id(1) == 0)
    def _():
        t_ref[...] = jnp.dot(_silu(cond_ref[...]), down_ref[...], precision=_HIGHEST,
                             preferred_element_type=_F32)

    o_ref[...] = jnp.dot(t_ref[...], up_ref[...], precision=_HIGHEST, preferred_element_type=_F32)


def ada_all(cond8, ada_down, ada_up):
    depth, d, r = ada_down.shape
    n = ada_up.shape[-1]
    tn = _pick_tile(n, 2048, LANES)
    return pl.pallas_call(
        _ada_kernel,
        grid=(depth, n // tn),
        in_specs=[pl.BlockSpec((8, d), lambda l, j: (0, 0)),
                  pl.BlockSpec((None, d, r), lambda l, j: (l, 0, 0)),
                  pl.BlockSpec((None, r, tn), lambda l, j: (l, 0, j))],
        out_specs=pl.BlockSpec((None, 8, tn), lambda l, j: (l, 0, j)),
        out_shape=jax.ShapeDtypeStruct((depth, 8, n), _F32),
        scratch_shapes=[pltpu.VMEM((8, r), _F32)],
        compiler_params=_params(("arbitrary", "arbitrary")),
        name="ada",
    )(cond8, ada_down, ada_up)


def _resmod_kernel(*refs, has_y, has_mod):
    it = iter(refs)
    x_ref = next(it)
    if has_y:
        y_ref, g_ref = next(it), next(it)
    if has_mod:
        sh_ref, sc_ref = next(it), next(it)
    if has_y:
        xo_ref = next(it)
    if has_mod:
        h_ref = next(it)
    x = x_ref[...]
    if has_y:
        x = x + g_ref[...] * y_ref[...]
        xo_ref[...] = x
    if has_mod:
        ms = jnp.mean(x * x, axis=-1, keepdims=True)
        h = x * lax.rsqrt(ms + EPS)
        h_ref[...] = (h * (1.0 + sc_ref[...]) + sh_ref[...]).astype(h_ref.dtype)


def resmod(xa, n_lat, y=None, gate=None, shift=None, scale=None, m_rows=None):
    m = xa.shape[0] if m_rows is None else m_rows
    d = xa.shape[1]
    tr = ROW_TILE
    assert m % tr == 0 and n_lat % tr == 0
    has_y, has_mod = y is not None, shift is not None
    n_lat_blocks = n_lat // tr
    row = pl.BlockSpec((tr, d), lambda i: (i, 0))
    vec = pl.BlockSpec((None, 1, d), lambda i: (jnp.where(i >= n_lat_blocks, 1, 0), 0, 0))
    args, in_specs, out_shape, out_specs = [xa], [row], [], []
    if has_y:
        args += [y, gate]
        in_specs += [row, vec]
        out_shape.append(jax.ShapeDtypeStruct((m, d), _F32))
        out_specs.append(row)
    if has_mod:
        args += [shift, scale]
        in_specs += [vec, vec]
        out_shape.append(jax.ShapeDtypeStruct((m, d), _BF16))
        out_specs.append(row)
    outs = pl.pallas_call(
        functools.partial(_resmod_kernel, has_y=has_y, has_mod=has_mod),
        grid=(m // tr,),
        in_specs=in_specs,
        out_specs=out_specs,
        out_shape=out_shape,
        compiler_params=_params(("parallel",)),
        name="resmod",
    )(*args)
    return outs


def _convglu_kernel(g_ref, u_ref, gp_ref, gn_ref, w_ref, b_ref, o_ref, *, tr, n_lat_blocks, n_blocks):
    i = pl.program_id(0)
    g = g_ref[...].astype(_F32)
    hs = SUBLANES_BF16
    starts_seq = jnp.logical_or(i == 0, i == n_lat_blocks)
    ends_seq = jnp.logical_or(i == n_lat_blocks - 1, i == n_blocks - 1)
    prev_row = jnp.where(starts_seq, 0.0, gp_ref[hs - 1:hs, :].astype(_F32))
    next_row = jnp.where(ends_seq, 0.0, gn_ref[0:1, :].astype(_F32))
    slab = lax.broadcasted_iota(jnp.int32, (8, g.shape[1]), 0)
    g_dn = pltpu.roll(g, 1, axis=0)
    g_up = pltpu.roll(g, tr - 1, axis=0)
    g_prev = jnp.concatenate([jnp.where(slab == 0, prev_row, g_dn[0:8]), g_dn[8:]], axis=0)
    g_next = jnp.concatenate([g_up[:tr - 8], jnp.where(slab == 7, next_row, g_up[tr - 8:])], axis=0)
    w = w_ref[...]
    z = g_prev * w[0:1, :] + g * w[1:2, :] + g_next * w[2:3, :] + b_ref[...]
    c0 = math.sqrt(2.0 / math.pi)
    t = jnp.tanh(z * (c0 + (c0 * 0.044715) * (z * z)))
    hz = 0.5 * z
    o_ref[...] = ((hz + hz * t) * u_ref[...].astype(_F32)).astype(o_ref.dtype)


def convglu(gate, up, conv_w, conv_b, n_lat):
    m, f_pad = gate.shape
    tr = ROW_TILE
    tc = _pick_tile(f_pad, 1408, LANES)
    hs = SUBLANES_BF16
    n_blocks = m // tr
    last_halo = m // hs - 1
    return pl.pallas_call(
        functools.partial(_convglu_kernel, tr=tr, n_lat_blocks=n_lat // tr, n_blocks=n_blocks),
        grid=(n_blocks, f_pad // tc),
        in_specs=[pl.BlockSpec((tr, tc), lambda i, j: (i, j)),
                  pl.BlockSpec((tr, tc), lambda i, j: (i, j)),
                  pl.BlockSpec((hs, tc), lambda i, j: (jnp.maximum(i * (tr // hs) - 1, 0), j)),
                  pl.BlockSpec((hs, tc), lambda i, j: (jnp.minimum((i + 1) * (tr // hs), last_halo), j)),
                  pl.BlockSpec((8, tc), lambda i, j: (0, j)),
                  pl.BlockSpec((1, tc), lambda i, j: (0, j))],
        out_specs=pl.BlockSpec((tr, tc), lambda i, j: (i, j)),
        out_shape=jax.ShapeDtypeStruct((m, f_pad), _BF16),
        compiler_params=_params(("parallel", "arbitrary")),
        name="convglu",
    )(gate, up, gate, gate, conv_w, conv_b)


def _seq_block_index(c, n_lat_blocks, n_ctx_blocks, reverse):
    if reverse:
        ctx = n_lat_blocks + n_ctx_blocks - 1 - c
        lat = n_lat_blocks - 1 - (c - n_ctx_blocks)
    else:
        ctx = n_lat_blocks + c
        lat = c - n_ctx_blocks
    return jnp.where(c < n_ctx_blocks, ctx, lat)


GLA_HEADS_PER_STEP = 2


def _gla_kernel(q_ref, k_ref, v_ref, a_ref, w2_ref, ba_ref, o_ref, s_ref, *, reverse, rank, scale, dk, dv):
    C, SUB, TB = GLA_CHUNK, GLA_SUB, SEQ_BLOCK
    HALF = SUB // 2
    n_chunks = TB // C
    n_sub = C // SUB

    @pl.when(pl.program_id(1) == 0)
    def _():
        s_ref[...] = jnp.zeros_like(s_ref)

    lane0 = rank if reverse else 0
    a = a_ref[:, lane0:lane0 + rank]
    z = jnp.dot(a, w2_ref[...], precision=_HIGHEST, preferred_element_type=_F32) + ba_ref[...]
    la = (jnp.minimum(z, 0.0) - jnp.log1p(jnp.exp(-jnp.abs(z)))) * (1.0 / GLA_TAU)
    t_idx = lax.broadcasted_iota(jnp.int32, (TB, TB), 0)
    u_idx = lax.broadcasted_iota(jnp.int32, (TB, TB), 1)
    same_chunk = (t_idx // C) == (u_idx // C)
    along = (u_idx >= t_idx) if reverse else (u_idx <= t_idx)
    cumsum_op = jnp.where(jnp.logical_and(same_chunk, along), 1.0, 0.0).astype(_F32)
    b_all = jnp.dot(cumsum_op, la, precision=_HIGHEST, preferred_element_type=_F32)
    sel_t = lax.broadcasted_iota(jnp.int32, (TB, n_chunks * LANES), 0) // C
    sel_c = lax.broadcasted_iota(jnp.int32, (TB, n_chunks * LANES), 1) // LANES
    chunk_sel = jnp.where(sel_t == sel_c, 1.0, 0.0).astype(_F32)
    tot_all = lax.dot_general(la, chunk_sel, (((0,), (0,)), ((), ())), precision=_HIGHEST,
                              preferred_element_type=_F32)
    key_idx = lax.broadcasted_iota(jnp.int32, (HALF, C), 1)
    half_rows = lax.broadcasted_iota(jnp.int32, (HALF, 1), 0)

    for hh in range(s_ref.shape[0]):
        _gla_head(q_ref, k_ref, v_ref, o_ref, s_ref, hh, b_all[:, hh * dk:(hh + 1) * dk],
                  tot_all[hh * dk:(hh + 1) * dk], key_idx, half_rows, reverse=reverse, scale=scale, dk=dk, dv=dv)


def _gla_head(q_ref, k_ref, v_ref, o_ref, s_ref, hh, b_all, tot_all, key_idx, half_rows, *, reverse, scale, dk, dv):
    C, SUB, TB = GLA_CHUNK, GLA_SUB, SEQ_BLOCK
    HALF = SUB // 2
    n_chunks = TB // C
    n_sub = C // SUB
    q_all = q_ref[:, hh * dk:(hh + 1) * dk] * scale
    k_all = k_ref[:, hh * dk:(hh + 1) * dk]
    s = s_ref[hh]
    for ci in range(n_chunks):
        cidx = (n_chunks - 1 - ci) if reverse else ci
        rows = slice(cidx * C, (cidx + 1) * C)
        b, q, k, v = b_all[rows], q_all[rows], k_all[rows], v_ref[rows, hh * dv:(hh + 1) * dv]
        o = jnp.dot((q * jnp.exp(b)).astype(_BF16), s.astype(_BF16), preferred_element_type=_F32)

        a_rows = []
        for i in range(n_sub):
            lo = i * SUB
            bi, qi, ki = b[lo:lo + SUB], q[lo:lo + SUB], k[lo:lo + SUB]
            has_earlier = (i < n_sub - 1) if reverse else (i > 0)
            if has_earlier:
                ref_row = lo + SUB - 1 if reverse else lo
                bref = b[ref_row:ref_row + 1]
                qe = qi * jnp.exp(bi - bref)
                ke = k * jnp.exp(jnp.minimum(bref - b, 0.0))
                off = lax.dot_general(qe.astype(_BF16), ke.astype(_BF16), (((1,), (1,)), ((), ())),
                                      preferred_element_type=_F32)
                earlier = (key_idx >= lo + SUB) if reverse else (key_idx < lo)
            for hf in range(2):
                h0 = hf * HALF
                bt, qt = bi[h0:h0 + HALF], qi[h0:h0 + HALF]
                ah = jnp.where(earlier, off[h0:h0 + HALF], 0.0) if has_earlier else jnp.zeros((HALF, C), _F32)
                s_range = range(h0, SUB) if reverse else range(0, h0 + HALF)
                for s_i in s_range:
                    dlt = bt - bi[s_i:s_i + 1]
                    if s_i // HALF == hf:
                        seen = (half_rows <= s_i - h0) if reverse else (half_rows >= s_i - h0)
                        dlt = jnp.where(seen, dlt, -jnp.inf)
                    wgt = qt * ki[s_i:s_i + 1] * jnp.exp(dlt)
                    col = jnp.sum(wgt, axis=1, keepdims=True)
                    ah = jnp.where(key_idx == lo + s_i, col, ah)
                a_rows.append(ah)
        amat = jnp.concatenate(a_rows, axis=0)
        o_ref[rows, hh * dv:(hh + 1) * dv] = o + jnp.dot(amat.astype(_BF16), v, preferred_element_type=_F32)

        end_row = 0 if reverse else C - 1
        kd = (k * jnp.exp(b[end_row:end_row + 1] - b)).astype(_BF16)
        dec = jnp.exp(tot_all[:, cidx * LANES:(cidx + 1) * LANES])
        dec = jnp.concatenate([dec] * (dv // LANES), axis=1)
        s = dec * s + lax.dot_general(kd, v, (((0,), (0,)), ((), ())), preferred_element_type=_F32)
    s_ref[hh] = s


def gla_core(qk, v, a, w_a2, b_a, n_lat, reverse):
    m = qk.shape[0]
    dk_all = qk.shape[1] // 2
    dv_all = v.shape[1]
    heads = GLA_HEADS
    dkh, dvh = dk_all // heads, dv_all // heads
    rank = w_a2.shape[1]
    hp = GLA_HEADS_PER_STEP
    assert heads % hp == 0
    groups = heads // hp
    tb = SEQ_BLOCK
    nlb, ncb = n_lat // tb, (m - n_lat) // tb
    d = 1 if reverse else 0
    blk = functools.partial(_seq_block_index, n_lat_blocks=nlb, n_ctx_blocks=ncb, reverse=reverse)
    return pl.pallas_call(
        functools.partial(_gla_kernel, reverse=reverse, rank=rank, scale=dkh ** -0.5, dk=dkh, dv=dvh),
        grid=(groups, nlb + ncb),
        in_specs=[pl.BlockSpec((tb, hp * dkh), lambda h, c: (blk(c), h)),
                  pl.BlockSpec((tb, hp * dkh), lambda h, c: (blk(c), groups + h)),
                  pl.BlockSpec((tb, hp * dvh), lambda h, c: (blk(c), h)),
                  pl.BlockSpec((tb, LANES), lambda h, c: (blk(c), 0)),
                  pl.BlockSpec((None, rank, hp * dkh), lambda h, c: (d, 0, h)),
                  pl.BlockSpec((None, 1, hp * dkh), lambda h, c: (d, 0, h))],
        out_specs=pl.BlockSpec((tb, hp * dvh), lambda h, c: (blk(c), h)),
        out_shape=jax.ShapeDtypeStruct((m, dv_all), _F32),
        scratch_shapes=[pltpu.VMEM((hp, dkh, dvh), _F32)],
        compiler_params=_params(("parallel", "arbitrary")),
        name="gla_rev" if reverse else "gla_fwd",
    )(qk, qk, v, a, w_a2, b_a)


def _gla_gate_kernel(of_ref, ob_ref, r_ref, br_ref, g_ref, o_ref):
    o = of_ref[...] + ob_ref[...]
    o = o * lax.rsqrt(jnp.mean(o * o, axis=-1, keepdims=True) + EPS) * g_ref[...]
    o_ref[...] = (_silu(r_ref[...] + br_ref[...]) * o).astype(o_ref.dtype)


def gla_gate(o_f, o_b, r, b_r, norm_g, m_rows):
    dv_all = o_f.shape[1]
    dvh = norm_g.shape[-1]
    tr = ROW_TILE
    blk = pl.BlockSpec((tr, dvh), lambda i, h: (i, h))
    return pl.pallas_call(
        _gla_gate_kernel,
        grid=(m_rows // tr, dv_all // dvh),
        in_specs=[blk, blk, blk,
                  pl.BlockSpec((1, dvh), lambda i, h: (0, h)),
                  pl.BlockSpec((1, dvh), lambda i, h: (0, 0))],
        out_specs=blk,
        out_shape=jax.ShapeDtypeStruct((m_rows, dv_all), _BF16),
        compiler_params=_params(("parallel", "arbitrary")),
        name="gla_gate",
    )(o_f, o_b, r, b_r, norm_g)


RET_HEADS_PER_STEP = 2


def _ret_kernel(q_ref, k_ref, v_ref, cos_ref, sin_ref, e_ref, o_ref, s_ref, *, reverse, scale, dk, dv):
    C = SEQ_BLOCK
    half = dk // 2

    @pl.when(pl.program_id(1) == 0)
    def _():
        s_ref[...] = jnp.zeros_like(s_ref)

    cos, sin = cos_ref[...], sin_ref[...]

    def rope(x):
        x1, x2 = x[:, :half], x[:, half:]
        return jnp.concatenate([x1 * cos - x2 * sin, x1 * sin + x2 * cos], axis=1)

    t_idx = lax.broadcasted_iota(jnp.int32, (C, C), 0).astype(_F32)
    s_idx = lax.broadcasted_iota(jnp.int32, (C, C), 1).astype(_F32)
    rel = (s_idx - t_idx) if reverse else (t_idx - s_idx)
    pos = lax.broadcasted_iota(jnp.int32, (C, dk), 0).astype(_F32)
    for hh in range(s_ref.shape[0]):
        lg = jnp.log1p(-jnp.exp2(-e_ref[hh]))[0:1, 0:1]
        q = rope(q_ref[:, hh * dk:(hh + 1) * dk] * scale)
        k = rope(k_ref[:, hh * dk:(hh + 1) * dk])
        v = v_ref[:, hh * dv:(hh + 1) * dv]
        decay = jnp.exp(jnp.where(rel >= 0, lg * rel, -jnp.inf))
        if reverse:
            q_dec, k_dec = jnp.exp(lg * (C - pos)), jnp.exp(lg * pos)
        else:
            q_dec, k_dec = jnp.exp(lg * (pos + 1.0)), jnp.exp(lg * (C - 1.0 - pos))
        s = s_ref[hh]
        scores = lax.dot_general(q.astype(_BF16), k.astype(_BF16), (((1,), (1,)), ((), ())),
                                 preferred_element_type=_F32) * decay
        o = jnp.dot(scores.astype(_BF16), v, preferred_element_type=_F32)
        o = o + jnp.dot((q * q_dec).astype(_BF16), s.astype(_BF16), preferred_element_type=_F32)
        o_ref[:, hh * dv:(hh + 1) * dv] = o
        state_dec = jnp.exp(lg * float(C))
        s_ref[hh] = state_dec * s + lax.dot_general((k * k_dec).astype(_BF16), v, (((0,), (0,)), ((), ())),
                                                    preferred_element_type=_F32)


def ret_core(qk, v, cos, sin, dec_b, n_lat, heads, reverse):
    m = qk.shape[0]
    dkh = qk.shape[1] // 2 // heads
    dvh = v.shape[1] // heads
    hp = RET_HEADS_PER_STEP
    assert heads % hp == 0
    groups = heads // hp
    tb = SEQ_BLOCK
    nlb, ncb = n_lat // tb, (m - n_lat) // tb
    d = 1 if reverse else 0
    blk = functools.partial(_seq_block_index, n_lat_blocks=nlb, n_ctx_blocks=ncb, reverse=reverse)
    return pl.pallas_call(
        functools.partial(_ret_kernel, reverse=reverse, scale=dkh ** -0.5, dk=dkh, dv=dvh),
        grid=(groups, nlb + ncb),
        in_specs=[pl.BlockSpec((tb, hp * dkh), lambda h, c: (blk(c), h)),
                  pl.BlockSpec((tb, hp * dkh), lambda h, c: (blk(c), groups + h)),
                  pl.BlockSpec((tb, hp * dvh), lambda h, c: (blk(c), h)),
                  pl.BlockSpec((tb, dkh // 2), lambda h, c: (blk(c), 0)),
                  pl.BlockSpec((tb, dkh // 2), lambda h, c: (blk(c), 0)),
                  pl.BlockSpec((None, hp, 8, LANES), lambda h, c: (d, h, 0, 0))],
        out_specs=pl.BlockSpec((tb, hp * dvh), lambda h, c: (blk(c), h)),
        out_shape=jax.ShapeDtypeStruct((m, v.shape[1]), _F32),
        scratch_shapes=[pltpu.VMEM((hp, dkh, dvh), _F32)],
        compiler_params=_params(("parallel", "arbitrary")),
        name="ret_rev" if reverse else "ret_fwd",
    )(qk, qk, v, cos, sin, dec_b)


def _ret_gate_kernel(of_ref, ob_ref, g_ref, w_ref, b_ref, o_ref):
    o = of_ref[...] + ob_ref[...]
    mu = jnp.mean(o, axis=-1, keepdims=True)
    oc = o - mu
    var = jnp.mean(oc * oc, axis=-1, keepdims=True)
    o = oc * lax.rsqrt(var + EPS) * w_ref[...] + b_ref[...]
    o_ref[...] = (_silu(g_ref[...]) * o).astype(o_ref.dtype)


def ret_gate(o_f, o_b, g, gn_w, gn_b, heads, m_rows):
    dv_all = o_f.shape[1]
    dvh = dv_all // heads
    tr = ROW_TILE
    blk = pl.BlockSpec((tr, dvh), lambda i, h: (i, h))
    vec = pl.BlockSpec((1, dvh), lambda i, h: (0, h))
    return pl.pallas_call(
        _ret_gate_kernel,
        grid=(m_rows // tr, heads),
        in_specs=[blk, blk, blk, vec, vec],
        out_specs=blk,
        out_shape=jax.ShapeDtypeStruct((m_rows, dv_all), _BF16),
        compiler_params=_params(("parallel", "arbitrary")),
        name="ret_gate",
    )(o_f, o_b, g, gn_w, gn_b)


def _mla_norm_kernel(cq_ref, ckv_ref, qn_ref, kvn_ref, oq_ref, okv_ref):
    cq = cq_ref[...]
    oq_ref[...] = (cq * lax.rsqrt(jnp.mean(cq * cq, axis=-1, keepdims=True) + EPS) * qn_ref[...]).astype(oq_ref.dtype)
    ckv = ckv_ref[...]
    okv_ref[...] = (ckv * lax.rsqrt(jnp.mean(ckv * ckv, axis=-1, keepdims=True) + EPS)
                    * kvn_ref[...]).astype(okv_ref.dtype)


def mla_norm(down, q_norm, kv_norm):
    m = down.shape[0]
    qr, kvr = q_norm.shape[-1], kv_norm.shape[-1]
    assert qr % kvr == 0
    tr = ROW_TILE
    return pl.pallas_call(
        _mla_norm_kernel,
        grid=(m // tr,),
        in_specs=[pl.BlockSpec((tr, qr), lambda i: (i, 0)),
                  pl.BlockSpec((tr, kvr), lambda i: (i, qr // kvr)),
                  pl.BlockSpec((1, qr), lambda i: (0, 0)),
                  pl.BlockSpec((1, kvr), lambda i: (0, 0))],
        out_specs=[pl.BlockSpec((tr, qr), lambda i: (i, 0)), pl.BlockSpec((tr, kvr), lambda i: (i, 0))],
        out_shape=[jax.ShapeDtypeStruct((m, qr), _BF16), jax.ShapeDtypeStruct((m, kvr), _BF16)],
        compiler_params=_params(("parallel",)),
        name="mla_norm",
    )(down, down, q_norm, kv_norm)


def _swap_rope_halves(x):
    lane = lax.broadcasted_iota(jnp.int32, x.shape, 1)
    return jnp.where((lane & 32) == 0, pltpu.roll(x, LANES - 32, axis=1), pltpu.roll(x, 32, axis=1))


def _mla_prep_kernel(qn_ref, qr_ref, kn_ref, kr_ref, v_ref, c_ref, s_ref, gqn_ref, gqr_ref, gkn_ref, gkr_ref,
                     oq_ref, ok_ref, ov_ref, *, qk_dim, scale, dv):
    cosv, sinv = c_ref[...], s_ref[...]
    lane = lax.broadcasted_iota(jnp.int32, cosv.shape, 1)
    low = lane < MLA_ROPE
    qr = qr_ref[...]
    kr = kr_ref[...]
    kr_sq = jnp.sum(kr * kr, axis=-1, keepdims=True)
    kr_g = kr * gkr_ref[...]
    kr_rot = kr_g * cosv + _swap_rope_halves(kr_g) * sinv
    for hh in range(2):
        qn = qn_ref[:, hh * MLA_NOPE:(hh + 1) * MLA_NOPE]
        mine = low if hh == 0 else jnp.logical_not(low)
        q_sq = jnp.sum(qn * qn, axis=-1, keepdims=True) + jnp.sum(jnp.where(mine, qr * qr, 0.0), axis=-1, keepdims=True)
        q_rr = lax.rsqrt(q_sq * (1.0 / qk_dim) + EPS) * scale
        qr_g = qr * q_rr * gqr_ref[...]
        qr_rot = qr_g * cosv + _swap_rope_halves(qr_g) * sinv
        if hh == 1:
            qr_rot = pltpu.roll(qr_rot, MLA_ROPE, axis=1)
        q_full = jnp.concatenate([qn * q_rr * gqn_ref[...], jnp.where(low, qr_rot, 0.0)], axis=1)
        oq_ref[hh] = q_full.T.astype(oq_ref.dtype)

        kn = kn_ref[:, hh * MLA_NOPE:(hh + 1) * MLA_NOPE]
        k_sq = jnp.sum(kn * kn, axis=-1, keepdims=True) + kr_sq
        k_rr = lax.rsqrt(k_sq * (1.0 / qk_dim) + EPS)
        ok_ref[hh, :, 0:MLA_NOPE] = (kn * k_rr * gkn_ref[...]).astype(ok_ref.dtype)
        ok_ref[hh, :, MLA_NOPE:2 * MLA_NOPE] = jnp.where(low, kr_rot * k_rr, 0.0).astype(ok_ref.dtype)

        ov_ref[hh, 0:dv, :] = v_ref[:, hh * dv:(hh + 1) * dv].astype(_F32).T.astype(ov_ref.dtype)
        ov_ref[hh, dv:, :] = jnp.ones((ov_ref.shape[1] - dv, ov_ref.shape[2]), ov_ref.dtype)


def mla_prep(qraw, knope, down, kr_block, v, cos2, sin2, q_gain, k_gain, heads):
    m = qraw.shape[0]
    qk_dim = MLA_NOPE + MLA_ROPE
    dv = v.shape[1] // heads
    assert dv == MLA_NOPE
    tr = _pick_tile(m, 1024, LANES)
    nope_blocks = heads * MLA_NOPE // (2 * MLA_NOPE)
    gqn, gkn = q_gain[None, :MLA_NOPE], k_gain[None, :MLA_NOPE]
    gqr = jnp.tile(q_gain[MLA_NOPE:], 2)[None]
    gkr = jnp.tile(k_gain[MLA_NOPE:], 2)[None]
    vec = pl.BlockSpec((1, LANES), lambda i, p: (0, 0))
    dve = dv + SUBLANES_BF16
    return pl.pallas_call(
        functools.partial(_mla_prep_kernel, qk_dim=qk_dim, scale=qk_dim ** -0.5 * math.log2(math.e), dv=dv),
        grid=(m // tr, heads // 2),
        in_specs=[pl.BlockSpec((tr, 2 * MLA_NOPE), lambda i, p: (i, p)),
                  pl.BlockSpec((tr, LANES), lambda i, p: (i, 2 * nope_blocks + p)),
                  pl.BlockSpec((tr, 2 * MLA_NOPE), lambda i, p: (i, p)),
                  pl.BlockSpec((tr, LANES), lambda i, p: (i, kr_block)),
                  pl.BlockSpec((tr, 2 * dv), lambda i, p: (i, p)),
                  pl.BlockSpec((tr, LANES), lambda i, p: (i, 0)),
                  pl.BlockSpec((tr, LANES), lambda i, p: (i, 0)),
                  vec, vec, vec, vec],
        out_specs=[pl.BlockSpec((2, 2 * MLA_NOPE, tr), lambda i, p: (p, 0, i)),
                   pl.BlockSpec((2, tr, 2 * MLA_NOPE), lambda i, p: (p, i, 0)),
                   pl.BlockSpec((2, dve, tr), lambda i, p: (p, 0, i))],
        out_shape=[jax.ShapeDtypeStruct((heads, 2 * MLA_NOPE, m), _BF16),
                   jax.ShapeDtypeStruct((heads, m, 2 * MLA_NOPE), _BF16),
                   jax.ShapeDtypeStruct((heads, dve, m), _BF16)],
        compiler_params=_params(("parallel", "arbitrary")),
        name="mla_prep",
    )(qraw, qraw, knope, down, v, cos2, sin2, gqn, gqr, gkn, gkr)


def _attn_kernel(qt_ref, k_ref, vt_ref, o_ref, acc_ref, *, tkv, n_kv, dv):
    tq = qt_ref.shape[1]
    qt = qt_ref[...]
    acc_ref[...] = jnp.zeros_like(acc_ref)
    m_run = jnp.full((1, tq), -jnp.inf, _F32)
    for j in range(n_kv):
        rows = slice(j * tkv, (j + 1) * tkv)
        s = jnp.dot(k_ref[rows, :], qt, preferred_element_type=_F32)
        m_new = jnp.maximum(m_run, jnp.max(s, axis=0, keepdims=True))
        alpha = jnp.exp2(m_run - m_new)
        p = jnp.exp2(s - m_new).astype(_BF16)
        acc_ref[...] = alpha * acc_ref[...] + jnp.dot(vt_ref[:, rows], p, preferred_element_type=_F32)
        m_run = m_new
    acc = acc_ref[...]
    o_ref[...] = (acc[:dv] / acc[dv:dv + 1]).T.astype(o_ref.dtype)


def attention(qt_hm, k_hm, vt_hm, q_row0, n_q, kv_row0, n_kv_rows):
    heads, dqk, _ = qt_hm.shape
    dve = vt_hm.shape[1]
    dv = dve - SUBLANES_BF16
    tq = _pick_tile(n_q, 2048, LANES)
    tkv = _pick_tile(n_kv_rows, 384, LANES)
    assert q_row0 % tq == 0 and kv_row0 % n_kv_rows == 0
    q0, kv0 = q_row0 // tq, kv_row0 // n_kv_rows
    return pl.pallas_call(
        functools.partial(_attn_kernel, tkv=tkv, n_kv=n_kv_rows // tkv, dv=dv),
        grid=(heads, n_q // tq),
        in_specs=[pl.BlockSpec((None, dqk, tq), lambda h, i: (h, 0, q0 + i)),
                  pl.BlockSpec((None, n_kv_rows, dqk), lambda h, i: (h, kv0, 0)),
                  pl.BlockSpec((None, dve, n_kv_rows), lambda h, i: (h, 0, kv0))],
        out_specs=pl.BlockSpec((tq, dv), lambda h, i: (i, h)),
        out_shape=jax.ShapeDtypeStruct((n_q, heads * dv), _BF16),
        scratch_shapes=[pltpu.VMEM((dve, tq), _F32)],
        compiler_params=_params(("parallel", "arbitrary")),
        name="attn",
    )(qt_hm, k_hm, vt_hm)


def _axial_rope(n_tokens, dim):
    rows = n_tokens // GRID_W
    quarter = dim // 4
    inv_freq = ROPE_BASE ** (-jnp.arange(quarter, dtype=_F32) / quarter)
    row = jnp.repeat(jnp.arange(rows, dtype=_F32), GRID_W)
    col = jnp.tile(jnp.arange(GRID_W, dtype=_F32), rows)
    ang = jnp.concatenate([row[:, None] * inv_freq, col[:, None] * inv_freq], axis=-1)
    return jnp.cos(ang), jnp.sin(ang)


def _with_ctx_identity(cos, sin, n_ctx):
    return (jnp.concatenate([cos, jnp.ones((n_ctx, cos.shape[1]), _F32)], axis=0),
            jnp.concatenate([sin, jnp.zeros((n_ctx, sin.shape[1]), _F32)], axis=0))


def _pad_last(w, width):
    return w if width == w.shape[-1] else jnp.pad(w, [(0, 0)] * (w.ndim - 1) + [(0, width - w.shape[-1])])


def _round_up(n, mult):
    return (n + mult - 1) // mult * mult


def _gla_layer(h, j, w_in, w_a, w_a2, b_a, b_r, norm_g, w_out, n_lat, m_out):
    dk = w_a2.shape[-1]
    dv = b_r.shape[-1]
    rank = w_a2.shape[1]
    assert 2 * rank <= LANES
    qk = mm(h, w_in, j, _F32, col0=0, n_cols=2 * dk, name="gla_qk")
    v = mm(h, w_in, j, _BF16, col0=2 * dk, n_cols=dv, name="gla_v")
    r = mm(h, w_in, j, _F32, m_rows=m_out, col0=2 * dk + dv, n_cols=dv, name="gla_r")
    a = mm(h, w_a, j, _F32, name="gla_a")
    o_f = gla_core(qk, v, a, w_a2, b_a[:, None, :], n_lat, reverse=False)
    o_b = gla_core(qk, v, a, w_a2, b_a[:, None, :], n_lat, reverse=True)
    gated = gla_gate(o_f, o_b, r, b_r[None], norm_g[None], m_out)
    return mm(gated, w_out, j, _F32, name="gla_out")


def _ret_layer(h, j, w_in, decay, gn_w, gn_b, w_out, rope, n_lat, m_out):
    heads = decay.shape[-1]
    dv = gn_w.shape[-1]
    dk = (w_in.shape[-1] - 2 * dv) // 2
    qk = mm(h, w_in, j, _F32, col0=0, n_cols=2 * dk, name="ret_qk")
    v = mm(h, w_in, j, _BF16, col0=2 * dk, n_cols=dv, name="ret_v")
    g = mm(h, w_in, j, _F32, m_rows=m_out, col0=2 * dk + dv, n_cols=dv, name="ret_g")
    dec_b = jnp.broadcast_to(decay.astype(_F32)[:, :, None, None], (2, heads, 8, LANES))
    cos, sin = rope
    o_f = ret_core(qk, v, cos, sin, dec_b, n_lat, heads, reverse=False)
    o_b = ret_core(qk, v, cos, sin, dec_b, n_lat, heads, reverse=True)
    gated = ret_gate(o_f, o_b, g, gn_w[None], gn_b[None], heads, m_out)
    return mm(gated, w_out, j, _F32, name="ret_out")


def _mla_weights(w_down, w_uq, w_ukv, heads, qk_dim):
    n, q_rank, kv_rank = w_uq.shape[0], w_uq.shape[1], w_ukv.shape[1]
    w_uq4 = w_uq.astype(_BF16).reshape(n, q_rank, heads, qk_dim)
    w_uq_p = jnp.concatenate([w_uq4[..., :MLA_NOPE].reshape(n, q_rank, heads * MLA_NOPE),
                              w_uq4[..., MLA_NOPE:].reshape(n, q_rank, heads * MLA_ROPE)], axis=-1)
    w_ukv4 = w_ukv.astype(_BF16).reshape(n, kv_rank, heads, -1)
    w_uk = w_ukv4[..., :MLA_NOPE].reshape(n, kv_rank, heads * MLA_NOPE)
    w_uv = w_ukv4[..., MLA_NOPE:].reshape(n, kv_rank, -1)
    w_dn = _pad_last(w_down.astype(_BF16), _round_up(w_down.shape[-1], LANES))
    return w_dn, w_uq_p, w_uk, w_uv


def _mla_layer(h, j, w_dn, q_norm, kv_norm, w_uq_p, w_uk, w_uv, q_gain, k_gain, w_out, rope2, n_lat, need_ctx):
    m = h.shape[0]
    n_ctx = m - n_lat
    q_rank, kv_rank = q_norm.shape[-1], kv_norm.shape[-1]
    qk_dim = q_gain.shape[-1]
    heads = w_uk.shape[-1] // MLA_NOPE
    assert qk_dim == MLA_NOPE + MLA_ROPE and heads % 2 == 0 and (q_rank + kv_rank) % LANES == 0
    down = mm(h, w_dn, j, _F32, name="mla_down")
    cqn, ckvn = mla_norm(down, q_norm[None], kv_norm[None])
    qraw = mm(cqn, w_uq_p, j, _F32, name="mla_uq")
    knope = mm(ckvn, w_uk, j, _F32, name="mla_uk")
    v = mm(ckvn, w_uv, j, _BF16, name="mla_uv")
    cos2, sin2 = rope2
    qt_hm, k_hm, vt_hm = mla_prep(qraw, knope, down, (q_rank + kv_rank) // LANES, v, cos2, sin2, q_gain, k_gain,
                                  heads)
    o_lat = attention(qt_hm, k_hm, vt_hm, 0, n_lat, 0, m)
    if need_ctx:
        o_ctx = attention(qt_hm, k_hm, vt_hm, n_lat, n_ctx, n_lat, n_ctx)
        o = jnp.concatenate([o_lat, o_ctx], axis=0)
    else:
        o = o_lat
    return mm(o, w_out, j, _F32, name="mla_out")


def kernel(x, c, ctx, c_ctx, ada_down, ada_up, gla_w_in, gla_w_a2, gla_b_a, gla_b_r, gla_norm, gla_w_out, ret_w_in, ret_decay, ret_gn_w, ret_gn_b, ret_w_out, mla_w_down, mla_q_norm, mla_kv_norm, mla_w_uq, mla_w_ukv, mla_q_gain, mla_k_gain, mla_w_out, ffn_w_in, ffn_conv_w, ffn_conv_b, ffn_w_out):
    assert x.shape[0] == 1 and ctx.shape[0] == 1 and c.shape[0] == 1
    n_lat, d = x.shape[1], x.shape[2]
    n_ctx = ctx.shape[1]
    depth = ada_down.shape[0]
    d_ff = ffn_conv_b.shape[-1]
    f_pad = _round_up(d_ff, 1024)

    xa = jnp.concatenate([x[0], ctx[0]], axis=0)
    cond8 = jnp.zeros((8, d), _F32).at[0].set(c[0]).at[1].set(c_ctx)
    mods = ada_all(cond8, ada_down, ada_up)[:, :2].reshape(depth, 2, 6, 1, d)

    ret_dkh = (ret_w_in.shape[-1] - 2 * ret_gn_w.shape[-1]) // 2 // ret_decay.shape[-1]
    ret_rope = _with_ctx_identity(*_axial_rope(n_lat, ret_dkh), n_ctx)
    cos_m, sin_m = _with_ctx_identity(*_axial_rope(n_lat, MLA_ROPE), n_ctx)
    mla_rope = (jnp.tile(cos_m, (1, 4)), jnp.tile(jnp.concatenate([-sin_m, sin_m], axis=1), (1, 2)))

    gla_in_b = gla_w_in.astype(_BF16)
    gla_rank = gla_w_a2.shape[2]
    gla_a_b = _pad_last(gla_w_in[:, :, gla_w_in.shape[-1] - 2 * gla_rank:].astype(_BF16), LANES)
    gla_out_b = gla_w_out.astype(_BF16)
    ret_in_b, ret_out_b = ret_w_in.astype(_BF16), ret_w_out.astype(_BF16)
    if mla_w_uq.shape[0]:
        mla_heads = mla_w_uq.shape[-1] // mla_q_gain.shape[-1]
        mla_dn_b, mla_uq_b, mla_uk_b, mla_uv_b = _mla_weights(mla_w_down, mla_w_uq, mla_w_ukv, mla_heads,
                                                              mla_q_gain.shape[-1])
        mla_out_b = mla_w_out.astype(_BF16)
    ffn_in_b = ffn_w_in.astype(_BF16)
    ffn_out_b = jnp.pad(ffn_w_out.astype(_BF16), ((0, 0), (0, f_pad - d_ff), (0, 0)))
    conv_w8 = jnp.pad(ffn_conv_w, ((0, 0), (0, 8 - ffn_conv_w.shape[1]), (0, f_pad - d_ff)))
    conv_b1 = _pad_last(ffn_conv_b, f_pad)[:, None, :]

    h = None
    for i in range(depth):
        kind, j = i % 3, i // 3
        need_ctx = i < depth - 1
        m_out = n_lat + n_ctx if need_ctx else n_lat
        sh1, sc1, g1, sh2, sc2, g2 = (mods[i, :, t] for t in range(6))
        if i == 0:
            (h,) = resmod(xa, n_lat, shift=sh1, scale=sc1)
        if kind == 0:
            y = _gla_layer(h, j, gla_in_b, gla_a_b, gla_w_a2[j], gla_b_a[j], gla_b_r[j], gla_norm[j], gla_out_b,
                           n_lat, m_out)
        elif kind == 1:
            y = _ret_layer(h, j, ret_in_b, ret_decay[j], ret_gn_w[j], ret_gn_b[j], ret_out_b, ret_rope,
                           n_lat, m_out)
        else:
            y = _mla_layer(h, j, mla_dn_b, mla_q_norm[j], mla_kv_norm[j], mla_uq_b, mla_uk_b, mla_uv_b,
                           mla_q_gain[j], mla_k_gain[j], mla_out_b, mla_rope, n_lat, need_ctx)
        xa, h = resmod(xa, n_lat, y=y, gate=g1, shift=sh2, scale=sc2, m_rows=m_out)

        gate, up = ffn_in(h, ffn_in_b, i, d_ff, f_pad, m_out)
        act = convglu(gate, up, conv_w8[i], conv_b1[i], n_lat if need_ctx else m_out)
        y = mm(act, ffn_out_b, i, _F32, name="ffn_out")
        if i + 1 < depth:
            sh1n, sc1n = mods[i + 1, :, 0], mods[i + 1, :, 1]
            xa, h = resmod(xa, n_lat, y=y, gate=g2, shift=sh1n, scale=sc1n, m_rows=m_out)
        else:
            (xa,) = resmod(xa, n_lat, y=y, gate=g2, m_rows=m_out)
    return xa[:n_lat][None]
```
